```python
import math
import jax, jax.numpy as jnp
from jax import lax
import numpy as np

D_MODEL = 1024
BATCH = 8
SEQ = 2048
DEPTH = 1
DEC_BATCH = 128
DEC_SEQ = 8
PAST_LEN = 16384
PAGE_SIZE = 128

HGRN_HEADS = 4
HGRN_DK = 128
HGRN_DV = 128
GLA_HEADS = 4
GLA_DK = 64
GLA_DV = 128
GLA_GATE_RANK = 16
GLA_GATE_NORMALIZER = 16.0
CHUNK = 16
N_EXPERTS = 32
TOP_K = 4
D_FF = D_MODEL
SWIGLU_LIMIT = 7.0
SWIGLU_ALPHA = 1.702
EPS = 1e-6

HK = HGRN_HEADS * HGRN_DK
HV = HGRN_HEADS * HGRN_DV
GK = GLA_HEADS * GLA_DK
GV = GLA_HEADS * GLA_DV
SPLITS = (HK, HK, HV, HV, GK, GK, GV, GLA_GATE_RANK, GV, D_MODEL, D_MODEL)
P_TOTAL = HK + HK + HV + HV + GK + GK + GV + GLA_GATE_RANK + GV + D_MODEL + D_MODEL

kernel_name = "hgrn2_gla_gated_merge_moe_adaln_step"


def _rmsnorm(x, g):
    xf = x.astype(jnp.float32)
    r = lax.rsqrt(jnp.mean(xf * xf, axis=-1, keepdims=True) + EPS)
    return (xf * r).astype(x.dtype) * g


def _gated_linear_recurrence(q, k, v, lg, s0):
    B, T, H, DK = q.shape
    DV = v.shape[-1]
    C = CHUNK
    n = -(-T // C)
    pad = n * C - T
    f32 = jnp.float32
    if pad:
        pw = ((0, 0), (0, pad), (0, 0), (0, 0))
        q, k, v, lg = (jnp.pad(a, pw) for a in (q, k, v, lg))
    q = q.astype(f32).reshape(B, n, C, H, DK)
    k = k.astype(f32).reshape(B, n, C, H, DK)
    v = v.astype(f32).reshape(B, n, C, H, DV)
    b = jnp.cumsum(lg.astype(f32).reshape(B, n, C, H, DK), axis=2)
    causal = jnp.tril(jnp.ones((C, C), dtype=bool))[None, None, :, :, None, None]
    diff = b[:, :, :, None] - b[:, :, None, :]
    decay = jnp.exp(jnp.where(causal, diff, -jnp.inf))
    att = jnp.einsum('bnthd,bnshd,bntshd->bnhts', q, k, decay)
    o_intra = jnp.einsum('bnhts,bnshv->bnthv', att, v)
    b_last = b[:, :, -1]
    k_dec = k * jnp.exp(b_last[:, :, None] - b)
    u = jnp.einsum('bnchd,bnchv->bnhdv', k_dec, v)

    def step(s, inp):
        a, du = inp
        return a[..., None] * s + du, s

    s_final, s_starts = lax.scan(step, s0.astype(f32),
                                 (jnp.moveaxis(jnp.exp(b_last), 1, 0), jnp.moveaxis(u, 1, 0)))
    s_starts = jnp.moveaxis(s_starts, 0, 1)
    o_inter = jnp.einsum('bnchd,bnhdv->bnchv', q * jnp.exp(b), s_starts)
    o = (o_intra + o_inter).reshape(B, n * C, H, DV)[:, :T]
    return o, s_final


def _head_norm_gate(o, gain, gate):
    r = lax.rsqrt(jnp.mean(o * o, axis=-1, keepdims=True) + EPS)
    return o * r * gain.astype(jnp.float32) * jax.nn.silu(gate.astype(jnp.float32))


def _moe(h, r_w, r_b, w_gu, b_gu, w_dn, b_dn):
    logits = (h @ r_w + r_b).astype(jnp.float32)
    top_v, top_i = lax.top_k(logits, TOP_K)
    top_w = jax.nn.softmax(top_v, axis=-1)
    combine = jnp.sum(jax.nn.one_hot(top_i, N_EXPERTS, dtype=jnp.float32) * top_w[..., None], axis=-2)
    combine = combine.astype(h.dtype)
    out = jnp.zeros(h.shape[:-1] + (D_MODEL,), h.dtype)
    for e in range(N_EXPERTS):
        gu = h @ w_gu[e] + b_gu[e]
        gate = jnp.minimum(gu[..., :D_FF], SWIGLU_LIMIT)
        up = jnp.clip(gu[..., D_FF:], -SWIGLU_LIMIT, SWIGLU_LIMIT)
        act = (up + 1.0) * (gate * jax.nn.sigmoid(SWIGLU_ALPHA * gate))
        out = out + combine[..., e:e + 1] * (act @ w_dn[e] + b_dn[e])
    return out


def _trunk_layer(x, c, s_h, s_g, lb, ada_w, ada_b, n1, n2, w_in, hgrn_norm_g,
                 gla_w2, gla_b, gla_norm_g, w_bh, w_bg, w_out, r_w, r_b, w_gu, b_gu, w_dn, b_dn):
    B, T, _ = x.shape
    mod = (jax.nn.silu(c) @ ada_w + ada_b)[:, None, :]
    sh1, sc1, gt1, sh2, sc2, gt2 = jnp.split(mod, 6, axis=-1)

    h = _rmsnorm(x, n1) * (1.0 + sc1) + sh1
    z = h @ w_in
    idx = np.cumsum(np.array(SPLITS))[:-1].tolist()
    hq, hf, hi, hg, gq, gk, gv, gr, gg, ma, mb = jnp.split(z, idx, axis=-1)

    f = lb + (1.0 - lb) * jax.nn.sigmoid(hf.astype(jnp.float32))
    q_h = (jax.nn.silu(hq.astype(jnp.float32)) * HGRN_DK ** -0.5).reshape(B, T, HGRN_HEADS, HGRN_DK)
    k_h = (1.0 - f).reshape(B, T, HGRN_HEADS, HGRN_DK)
    lg_h = jnp.log(f).reshape(B, T, HGRN_HEADS, HGRN_DK)
    v_h = hi.reshape(B, T, HGRN_HEADS, HGRN_DV)
    o_h, s_h_new = _gated_linear_recurrence(q_h, k_h, v_h, lg_h, s_h)
    o_h = _head_norm_gate(o_h, hgrn_norm_g, hg.reshape(B, T, HGRN_HEADS, HGRN_DV))
    o_h = o_h.reshape(B, T, HV).astype(x.dtype)

    lg_g = jax.nn.log_sigmoid((gr @ gla_w2 + gla_b).astype(jnp.float32)) / GLA_GATE_NORMALIZER
    q_g = (gq.astype(jnp.float32) * GLA_DK ** -0.5).reshape(B, T, GLA_HEADS, GLA_DK)
    k_g = gk.reshape(B, T, GLA_HEADS, GLA_DK)
    v_g = gv.reshape(B, T, GLA_HEADS, GLA_DV)
    o_g, s_g_new = _gated_linear_recurrence(q_g, k_g, v_g, lg_g.reshape(B, T, GLA_HEADS, GLA_DK), s_g)
    o_g = _head_norm_gate(o_g, gla_norm_g, gg.reshape(B, T, GLA_HEADS, GLA_DV))
    o_g = o_g.reshape(B, T, GV).astype(x.dtype)

    merged = jax.nn.sigmoid(ma) * (o_h @ w_bh) + jax.nn.sigmoid(mb) * (o_g @ w_bg)
    x = x + gt1 * (merged @ w_out)

    h2 = _rmsnorm(x, n2) * (1.0 + sc2) + sh2
    x = x + gt2 * _moe(h2, r_w, r_b, w_gu, b_gu, w_dn, b_dn)
    return x, s_h_new, s_g_new


def setup_inputs(seed: int = 0) -> dict:
    key = jax.random.key(seed)
    ks = jax.random.split(key, 32)

    def nrm(k, shape, s):
        return jax.random.normal(k, shape, jnp.float32) * s

    D = D_MODEL
    return {
        "x_prompt": nrm(ks[0], (BATCH, SEQ, D), 1.0),
        "x_sample": nrm(ks[1], (DEC_BATCH, DEC_SEQ, D), 1.0),
        "c_prompt": nrm(ks[2], (BATCH, D), 1.0),
        "c_sample": nrm(ks[3], (DEC_BATCH, D), 1.0),
        "state_hgrn": nrm(ks[4], (DEPTH, DEC_BATCH, HGRN_HEADS, HGRN_DK, HGRN_DV), 0.1),
        "state_gla": nrm(ks[5], (DEPTH, DEC_BATCH, GLA_HEADS, GLA_DK, GLA_DV), 0.1),
        "ada_w": nrm(ks[6], (DEPTH, D, 6 * D), 0.5 * D ** -0.5),
        "ada_b": nrm(ks[7], (DEPTH, 6 * D), 0.01),
        "norm1_g": 1.0 + nrm(ks[8], (DEPTH, D), 0.02),
        "norm2_g": 1.0 + nrm(ks[9], (DEPTH, D), 0.02),
        "w_in": nrm(ks[10], (DEPTH, D, P_TOTAL), D ** -0.5),
        "hgrn_lb": 1.0 + nrm(ks[11], (DEPTH + 1, HK), 0.1),
        "hgrn_norm_g": 1.0 + nrm(ks[12], (DEPTH, HGRN_DV), 0.02),
        "gla_gate_w2": nrm(ks[13], (DEPTH, GLA_GATE_RANK, GK), GLA_GATE_RANK ** -0.5),
        "gla_gate_b": nrm(ks[14], (DEPTH, GK), 0.01),
        "gla_norm_g": 1.0 + nrm(ks[15], (DEPTH, GLA_DV), 0.02),
        "w_branch_h": nrm(ks[16], (DEPTH, HV, D), HV ** -0.5),
        "w_branch_g": nrm(ks[17], (DEPTH, GV, D), GV ** -0.5),
        "w_out": nrm(ks[18], (DEPTH, D, D), D ** -0.5),
        "router_w": nrm(ks[19], (DEPTH, D, N_EXPERTS), D ** -0.5),
        "router_b": nrm(ks[20], (DEPTH, N_EXPERTS), 0.01),
        "exp_w_gu": nrm(ks[21], (DEPTH, N_EXPERTS, D, 2 * D_FF), D ** -0.5),
        "exp_b_gu": nrm(ks[22], (DEPTH, N_EXPERTS, 2 * D_FF), 0.01),
        "exp_w_dn": nrm(ks[23], (DEPTH, N_EXPERTS, D_FF, D), D_FF ** -0.5),
        "exp_b_dn": nrm(ks[24], (DEPTH, N_EXPERTS, D), 0.01),
        "final_g": 1.0 + nrm(ks[25], (D,), 0.02),
    }


def reference(x_prompt, x_sample, c_prompt, c_sample, state_hgrn, state_gla, ada_w, ada_b,
              norm1_g, norm2_g, w_in, hgrn_lb, hgrn_norm_g, gla_gate_w2, gla_gate_b, gla_norm_g,
              w_branch_h, w_branch_g, w_out, router_w, router_b, exp_w_gu, exp_b_gu,
              exp_w_dn, exp_b_dn, final_g):
    lbs = jnp.cumsum(jax.nn.softmax(hgrn_lb.astype(jnp.float32), axis=0), axis=0)
    hp, hs = x_prompt, x_sample
    ph, pg, sh, sg = [], [], [], []
    for l in range(DEPTH):
        lw = (ada_w[l], ada_b[l], norm1_g[l], norm2_g[l], w_in[l], hgrn_norm_g[l],
              gla_gate_w2[l], gla_gate_b[l], gla_norm_g[l], w_branch_h[l], w_branch_g[l],
              w_out[l], router_w[l], router_b[l], exp_w_gu[l], exp_b_gu[l], exp_w_dn[l], exp_b_dn[l])
        z_h = jnp.zeros((BATCH, HGRN_HEADS, HGRN_DK, HGRN_DV), jnp.float32)
        z_g = jnp.zeros((BATCH, GLA_HEADS, GLA_DK, GLA_DV), jnp.float32)
        hp, s1, s2 = _trunk_layer(hp, c_prompt, z_h, z_g, lbs[l], *lw)
        hs, s3, s4 = _trunk_layer(hs, c_sample, state_hgrn[l], state_gla[l], lbs[l], *lw)
        ph.append(s1.astype(state_hgrn.dtype))
        pg.append(s2.astype(state_gla.dtype))
        sh.append(s3.astype(state_hgrn.dtype))
        sg.append(s4.astype(state_gla.dtype))
    y_prompt = _rmsnorm(hp, final_g)
    y_sample = _rmsnorm(hs, final_g)
    return (y_prompt, y_sample, jnp.stack(ph), jnp.stack(pg), jnp.stack(sh), jnp.stack(sg))
```

```python
import functools

import jax
import jax.numpy as jnp
from jax import lax
from jax.experimental import pallas as pl
from jax.experimental.pallas import tpu as pltpu

F32 = jnp.float32
BF16 = jnp.bfloat16
I32 = jnp.int32

D = 1024
HEADS = 4
HEAD_W = 128
GLA_DK = 64
GLA_RANK = 16
GLA_NORM = 16.0
N_EXPERTS = 32
TOP_K = 4
D_FF = 1024
SWIGLU_LIMIT = 7.0
SWIGLU_ALPHA = 1.702
EPS = 1e-6

BW = HEADS * HEAD_W
C_HQ, C_HF, C_HI, C_HG = 0, 512, 1024, 1536
C_GQ, C_GK, C_GV = 2048, 2560, 3072
C_GR = 3584
C_GG = 3712
C_MA, C_MB = 4224, 5248
PW = 6272

GROUP = 128
TM = 256
TOK_TILE = 256
VMEM_LIMIT = 56 * 1024 * 1024


def _dot(a, b):
    return jnp.dot(a, b, preferred_element_type=F32)


def _dot_nt(a, b):
    return lax.dot_general(a, b, (((1,), (1,)), ((), ())), preferred_element_type=F32)


def _dot_tn(a, b):
    return lax.dot_general(a, b, (((0,), (0,)), ((), ())), preferred_element_type=F32)


def _split3(x):
    h1 = x.astype(BF16)
    r1 = x - h1.astype(F32)
    h2 = r1.astype(BF16)
    r2 = r1 - h2.astype(F32)
    return h1, h2, r2.astype(BF16)


def _split2(x):
    h1 = x.astype(BF16)
    return h1, (x - h1.astype(F32)).astype(BF16)


def _dot3(a, b):
    ah, al = _split2(a)
    bh, bl = _split2(b)
    return _dot(ah, bh) + _dot(al, bh) + _dot(ah, bl)


def _sigmoid(x):
    return 1.0 / (1.0 + jnp.exp(-x))


def _silu(x):
    return x * _sigmoid(x)


def _log_sigmoid(x):
    return jnp.minimum(x, 0.0) - jnp.log(1.0 + jnp.exp(-jnp.abs(x)))


def _mod_kernel(c_ref, w_ref, b_ref, o_ref):
    c = c_ref[...]
    o_ref[...] = _dot3(_silu(c), w_ref[...]) + b_ref[...]


def _modulation(c_all, ada_w, ada_b):
    n = c_all.shape[0]
    bn = 1024
    return pl.pallas_call(
        _mod_kernel,
        grid=(6 * D // bn,),
        in_specs=[pl.BlockSpec((n, D), lambda i: (0, 0)),
                  pl.BlockSpec((D, bn), lambda i: (0, i)),
                  pl.BlockSpec((1, bn), lambda i: (0, i))],
        out_specs=pl.BlockSpec((n, bn), lambda i: (0, i)),
        out_shape=jax.ShapeDtypeStruct((n, 6 * D), F32),
        compiler_params=pltpu.CompilerParams(dimension_semantics=("arbitrary",),
                                             vmem_limit_bytes=VMEM_LIMIT),
        name="adaln_mod",
    )(c_all, ada_w, ada_b.reshape(1, 6 * D))


def _recurrence(q, k, lg, v, st_ref, lf_mat, tri_mask, blk_mask, nsg):
    l1, l2, l3 = _split3(lg)
    bb = _dot(lf_mat, l1) + _dot(lf_mat, l2) + _dot(lf_mat, l3)
    b = bb[:GROUP]
    b_last = bb[GROUP:]
    eb = jnp.exp(b)
    qd = (q * eb).astype(BF16)
    kd = (k * jnp.exp(-b)).astype(BF16)
    kdec = (k * jnp.exp(b_last - b)).astype(BF16)
    a = jnp.exp(b_last)
    vb = v.astype(BF16)
    outs = []
    for h in range(HEADS):
        cs = slice(h * HEAD_W, (h + 1) * HEAD_W)
        qd_h, kd_h, kdec_h, v_h = qd[:, cs], kd[:, cs], kdec[:, cs], vb[:, cs]
        att = jnp.where(tri_mask, _dot_nt(qd_h, kd_h), 0.0).astype(BF16)
        st = st_ref[h]
        if nsg == 1:
            q_blk, k_blk, a_row = qd_h, kdec_h, a[0:1, cs]
        else:
            zero = jnp.zeros((), BF16)
            q_blk = jnp.where(blk_mask, jnp.concatenate([qd_h] * nsg, axis=1), zero)
            k_blk = jnp.where(blk_mask, jnp.concatenate([kdec_h] * nsg, axis=1), zero)
            c = GROUP // nsg
            a_row = jnp.concatenate([a[n * c:n * c + 1, cs] for n in range(nsg)], axis=1)
        o = _dot(att, v_h) + _dot_nt(q_blk, st.astype(BF16))
        st_ref[h] = st * a_row + _dot_tn(v_h, k_blk)
        outs.append(o)
    return jnp.concatenate(outs, axis=1)


def _head_norm_gate(o, gain, gate):
    parts = []
    for h in range(HEADS):
        oh = o[:, h * HEAD_W:(h + 1) * HEAD_W]
        r = lax.rsqrt(jnp.mean(oh * oh, axis=-1, keepdims=True) + EPS)
        parts.append(oh * r * gain)
    return jnp.concatenate(parts, axis=1) * _silu(gate)


def _mixer_kernel(*refs, ns, tt, chunk, has_state, n_alias):
    (x_ref, mod_ref, n1_ref, win_ref, lb_ref, hng_ref, w2_ref, gb_ref, gng_ref,
     wbh_ref, wbg_ref, wout_ref, n2_ref, rwt_ref, rb_ref) = refs[:15]
    pos = 15
    if has_state:
        sh0_ref, sg0_ref = refs[pos:pos + 2]
        pos += 2
    pos += n_alias
    x1_ref, h2_ref, topi_ref, topw_ref, shout_ref, sgout_ref = refs[pos:pos + 6]
    z_ref, o_ref, sth_ref, stg_ref = refs[pos + 6:]

    rows = ns * tt
    n_groups = rows // GROUP
    nsg = GROUP // chunk
    j = pl.program_id(1)

    @pl.when(j == 0)
    def _():
        if has_state:
            for n in range(ns):
                for h in range(HEADS):
                    sth_ref[h, :, n * HEAD_W:(n + 1) * HEAD_W] = sh0_ref[0, n, h].T
                    g0 = jnp.concatenate(
                        [sg0_ref[0, n, h], jnp.zeros((HEAD_W - GLA_DK, HEAD_W), F32)], axis=0)
                    stg_ref[h, :, n * HEAD_W:(n + 1) * HEAD_W] = g0.T
        else:
            sth_ref[...] = jnp.zeros_like(sth_ref)
            stg_ref[...] = jnp.zeros_like(stg_ref)

    x = x_ref[...]
    mod = mod_ref[...]
    sh1, sc1, gt1 = mod[:, :, 0:D], mod[:, :, D:2 * D], mod[:, :, 2 * D:3 * D]
    sh2, sc2 = mod[:, :, 3 * D:4 * D], mod[:, :, 4 * D:5 * D]
    r = lax.rsqrt(jnp.mean(x * x, axis=-1, keepdims=True) + EPS)
    h = (x * r) * n1_ref[...] * (1.0 + sc1) + sh1
    hb = h.reshape(rows, D).astype(BF16)

    cw = 512
    for c0 in range(0, PW, cw):
        c1 = min(c0 + cw, PW)
        z_ref[:, c0:c1] = _dot(hb, win_ref[:, c0:c1])

    ri = lax.broadcasted_iota(I32, (GROUP, GROUP), 0)
    ci = lax.broadcasted_iota(I32, (GROUP, GROUP), 1)
    same = (ri // chunk) == (ci // chunk)
    tri_mask = same & (ci <= ri)
    lf_mat = jnp.concatenate([jnp.where(tri_mask, 1.0, 0.0), jnp.where(same, 1.0, 0.0)],
                             axis=0).astype(BF16)
    if nsg > 1:
        rb_i = lax.broadcasted_iota(I32, (GROUP, nsg * HEAD_W), 0)
        cb_i = lax.broadcasted_iota(I32, (GROUP, nsg * HEAD_W), 1)
        blk_mask = (rb_i // chunk) == (cb_i // HEAD_W)
    else:
        blk_mask = None

    lb_raw = lb_ref[...]
    lb_e = jnp.exp(lb_raw - jnp.max(lb_raw, axis=0, keepdims=True))
    lb = lb_e[0:1] / jnp.sum(lb_e, axis=0, keepdims=True)
    for g in range(n_groups):
        rs = slice(g * GROUP, (g + 1) * GROUP)
        hq = z_ref[rs, C_HQ:C_HQ + BW]
        f = lb + (1.0 - lb) * _sigmoid(z_ref[rs, C_HF:C_HF + BW])
        o_h = _recurrence(_silu(hq) * (HEAD_W ** -0.5), 1.0 - f, jnp.log(f),
                          z_ref[rs, C_HI:C_HI + BW], sth_ref, lf_mat, tri_mask, blk_mask, nsg)
        o_ref[rs, 0:BW] = o_h
        xg = _dot(z_ref[rs, C_GR:C_GR + 128].astype(BF16), w2_ref[...]) + gb_ref[...]
        o_g = _recurrence(z_ref[rs, C_GQ:C_GQ + BW] * (GLA_DK ** -0.5), z_ref[rs, C_GK:C_GK + BW],
                          _log_sigmoid(xg) / GLA_NORM, z_ref[rs, C_GV:C_GV + BW],
                          stg_ref, lf_mat, tri_mask, blk_mask, nsg)
        o_ref[rs, BW:2 * BW] = o_g

    on_h = _head_norm_gate(o_ref[:, 0:BW], hng_ref[...], z_ref[:, C_HG:C_HG + BW])
    on_g = _head_norm_gate(o_ref[:, BW:2 * BW], gng_ref[...], z_ref[:, C_GG:C_GG + BW])
    merged = (_sigmoid(z_ref[:, C_MA:C_MA + D]) * _dot(on_h.astype(BF16), wbh_ref[...])
              + _sigmoid(z_ref[:, C_MB:C_MB + D]) * _dot(on_g.astype(BF16), wbg_ref[...]))
    y = _dot(merged.astype(BF16), wout_ref[...])
    x1 = x + gt1 * y.reshape(ns, tt, D)
    x1_ref[...] = x1.reshape(rows, D)

    r2 = lax.rsqrt(jnp.mean(x1 * x1, axis=-1, keepdims=True) + EPS)
    h2 = ((x1 * r2) * n2_ref[...] * (1.0 + sc2) + sh2).reshape(rows, D)
    h2_ref[...] = h2
    hh, hl = _split2(h2)
    wh, wl = _split2(rwt_ref[...])
    logits = _dot_nt(wh, hh) + _dot_nt(wl, hh) + _dot_nt(wh, hl) + rb_ref[...]
    e_iota = lax.broadcasted_iota(I32, (N_EXPERTS, rows), 0).astype(F32)
    vals, idxs = [], []
    for _ in range(TOP_K):
        m = jnp.max(logits, axis=0, keepdims=True)
        i = jnp.min(jnp.where(logits == m, e_iota, float(N_EXPERTS)), axis=0, keepdims=True)
        vals.append(m)
        idxs.append(i)
        logits = jnp.where(e_iota == i, -jnp.inf, logits)
    es = [jnp.exp(v - vals[0]) for v in vals]
    inv = 1.0 / (es[0] + es[1] + es[2] + es[3])
    topi_ref[...] = jnp.concatenate(idxs, axis=0).astype(I32)
    w_t = jnp.concatenate([e * inv for e in es] + [jnp.zeros((128 - TOP_K, rows), F32)], axis=0)
    topw_ref[...] = w_t.T

    @pl.when(j == pl.num_programs(1) - 1)
    def _():
        for n in range(ns if nsg > 1 else 1):
            for hd in range(HEADS):
                shout_ref[0, n, hd] = sth_ref[hd, :, n * HEAD_W:(n + 1) * HEAD_W].T
                sgout_ref[0, n, hd] = stg_ref[hd, :, n * HEAD_W:(n + 1) * HEAD_W].T[:GLA_DK]


def _const_spec(shape):
    nd = len(shape)
    return pl.BlockSpec(shape, lambda s, j: (0,) * nd, pipeline_mode=pl.Buffered(1))


def _mixer(x, mod3, weights, states, alias_bufs, *, ns, tt, chunk, row_block0, n_tok_total):
    bsz, t_len, _ = x.shape
    rows = ns * tt
    n_sb, n_tt = bsz // ns, t_len // tt
    has_state = states is not None
    nss = ns if chunk < GROUP else 1
    assert (chunk == GROUP and ns == 1) or (rows == GROUP and tt == chunk)

    def row_blk(s, j):
        return (row_block0 + s * n_tt + j, 0)

    in_specs = [pl.BlockSpec((ns, tt, D), lambda s, j: (s, j, 0)),
                pl.BlockSpec((ns, 1, 6 * D), lambda s, j: (s, 0, 0))]
    in_specs += [_const_spec(w.shape) for w in weights]
    args = [x, mod3, *weights]
    if has_state:
        in_specs += [pl.BlockSpec((1, ns, HEADS, 128, 128), lambda s, j: (0, s, 0, 0, 0)),
                     pl.BlockSpec((1, ns, HEADS, GLA_DK, 128), lambda s, j: (0, s, 0, 0, 0))]
        args += list(states)
    aliases = {}
    if alias_bufs is not None:
        for k, buf in enumerate(alias_bufs):
            in_specs.append(pl.BlockSpec(memory_space=pl.ANY))
            aliases[len(args)] = k
            args.append(buf)
    n_alias = 0 if alias_bufs is None else len(alias_bufs)

    out_shape = [jax.ShapeDtypeStruct((n_tok_total, D), F32),
                 jax.ShapeDtypeStruct((n_tok_total, D), F32),
                 jax.ShapeDtypeStruct((TOP_K, n_tok_total), I32),
                 jax.ShapeDtypeStruct((n_tok_total, 128), F32),
                 jax.ShapeDtypeStruct((1, bsz, HEADS, 128, 128), F32),
                 jax.ShapeDtypeStruct((1, bsz, HEADS, GLA_DK, 128), F32)]
    out_specs = [pl.BlockSpec((rows, D), row_blk),
                 pl.BlockSpec((rows, D), row_blk),
                 pl.BlockSpec((TOP_K, rows), lambda s, j: (0, row_block0 + s * n_tt + j)),
                 pl.BlockSpec((rows, 128), row_blk),
                 pl.BlockSpec((1, ns, HEADS, 128, 128), lambda s, j: (0, s, 0, 0, 0)),
                 pl.BlockSpec((1, ns, HEADS, GLA_DK, 128), lambda s, j: (0, s, 0, 0, 0))]
    scratch = [pltpu.VMEM((rows, PW), F32),
               pltpu.VMEM((rows, 2 * BW), F32),
               pltpu.VMEM((HEADS, HEAD_W, nss * HEAD_W), F32),
               pltpu.VMEM((HEADS, HEAD_W, nss * HEAD_W), F32)]
    kern = functools.partial(_mixer_kernel, ns=ns, tt=tt, chunk=chunk,
                             has_state=has_state, n_alias=n_alias)
    return pl.pallas_call(
        kern,
        grid=(n_sb, n_tt),
        in_specs=in_specs,
        out_specs=out_specs,
        out_shape=out_shape,
        scratch_shapes=scratch,
        input_output_aliases=aliases,
        compiler_params=pltpu.CompilerParams(dimension_semantics=("arbitrary", "arbitrary"),
                                             vmem_limit_bytes=VMEM_LIMIT),
        name="mixer_state" if has_state else "mixer_prompt",
    )(*args)


def _route_kernel(topi_ref, pos_ref, te_ref, tv_ref, na_ref, rank_ref, *, n_tok, n_tile_pad):
    blk = 256
    nb = n_tok // blk
    e_iota = lax.broadcasted_iota(I32, (N_EXPERTS, blk), 0)
    upper = jnp.where(lax.broadcasted_iota(I32, (blk, blk), 0)
                      < lax.broadcasted_iota(I32, (blk, blk), 1), 1.0, 0.0).astype(BF16)

    def onehot(off):
        ti = topi_ref[:, pl.ds(off, blk)]
        oh = jnp.zeros((N_EXPERTS, blk), F32)
        for k in range(TOP_K):
            oh = oh + jnp.where(e_iota == ti[k:k + 1, :], 1.0, 0.0)
        return ti, oh

    def rank_body(jb, carry):
        off = pl.multiple_of(jb * blk, blk)
        _, oh = onehot(off)
        rank_ref[:, pl.ds(off, blk)] = _dot(oh.astype(BF16), upper) + carry
        return carry + jnp.sum(oh, axis=1, keepdims=True)

    counts = lax.fori_loop(0, nb, rank_body, jnp.zeros((N_EXPERTS, 1), F32))
    gsize = jnp.floor((counts + (TM - 1)) * (1.0 / TM)) * TM
    ee_r = lax.broadcasted_iota(I32, (N_EXPERTS, N_EXPERTS), 0)
    ee_c = lax.broadcasted_iota(I32, (N_EXPERTS, N_EXPERTS), 1)
    gsize_row = jnp.sum(jnp.where(ee_r == ee_c, gsize, 0.0), axis=0, keepdims=True)
    gstart = jnp.sum(jnp.where(ee_c < ee_r, gsize_row, 0.0), axis=1, keepdims=True)

    def pos_body(jb, _):
        off = pl.multiple_of(jb * blk, blk)
        ti, _unused = onehot(off)
        base = rank_ref[:, pl.ds(off, blk)] + gstart
        rows = [jnp.sum(jnp.where(e_iota == ti[k:k + 1, :], base, 0.0), axis=0, keepdims=True)
                for k in range(TOP_K)]
        pos_ref[:, pl.ds(off, blk)] = jnp.concatenate(rows, axis=0).astype(I32)
        return 0

    lax.fori_loop(0, nb, pos_body, 0)

    t0 = lax.broadcasted_iota(I32, (N_EXPERTS, n_tile_pad), 1).astype(F32) * TM
    te_iota = lax.broadcasted_iota(I32, (N_EXPERTS, n_tile_pad), 0)
    gend = gstart + gsize
    te = jnp.minimum(jnp.sum(jnp.where(gend <= t0, 1.0, 0.0), axis=0, keepdims=True),
                     N_EXPERTS - 1.0)
    sel = te_iota == te.astype(I32)
    left = jnp.sum(jnp.where(sel, counts - (t0 - gstart), 0.0), axis=0, keepdims=True)
    te_ref[...] = te.astype(I32)
    tv_ref[...] = jnp.clip(left, 0.0, float(TM)).astype(I32)
    total = jnp.sum(gsize, axis=0, keepdims=True) * (1.0 / TM)
    na_ref[...] = jnp.broadcast_to(total, (1, 128)).astype(I32)


def _route(topi_t, n_tile_pad):
    n_tok = topi_t.shape[1]
    kern = functools.partial(_route_kernel, n_tok=n_tok, n_tile_pad=n_tile_pad)
    return pl.pallas_call(
        kern,
        out_shape=[jax.ShapeDtypeStruct((TOP_K, n_tok), I32),
                   jax.ShapeDtypeStruct((1, n_tile_pad), I32),
                   jax.ShapeDtypeStruct((1, n_tile_pad), I32),
                   jax.ShapeDtypeStruct((1, 128), I32)],
        scratch_shapes=[pltpu.VMEM((N_EXPERTS, n_tok), F32)],
        compiler_params=pltpu.CompilerParams(vmem_limit_bytes=VMEM_LIMIT),
        name="route_plan",
    )(topi_t)


def _scatter_kernel(pos_ref, h2_ref, xs_ref, sem, *, n_tok):
    i = pl.program_id(0)
    base = i * TOK_TILE

    def row_copy(t, k):
        p = pos_ref[k * n_tok + base + t]
        return pltpu.make_async_copy(h2_ref.at[pl.ds(t, 1)], xs_ref.at[pl.ds(p, 1)], sem)

    def start_body(t, _):
        for k in range(TOP_K):
            row_copy(t, k).start()
        return 0

    lax.fori_loop(0, TOK_TILE, start_body, 0)

    def wait_body(t, _):
        for k in range(TOP_K):
            row_copy(t, k).wait()
        return 0

    lax.fori_loop(0, TOK_TILE, wait_body, 0)


def _scatter_rows(pos_flat, h2, n_rows_sorted):
    n_tok = h2.shape[0]
    kern = functools.partial(_scatter_kernel, n_tok=n_tok)
    return pl.pallas_call(
        kern,
        grid_spec=pltpu.PrefetchScalarGridSpec(
            num_scalar_prefetch=1,
            grid=(n_tok // TOK_TILE,),
            in_specs=[pl.BlockSpec((TOK_TILE, D), lambda i, pos: (i, 0))],
            out_specs=pl.BlockSpec(memory_space=pl.ANY),
            scratch_shapes=[pltpu.SemaphoreType.DMA(())]),
        out_shape=jax.ShapeDtypeStruct((n_rows_sorted, D), F32),
        compiler_params=pltpu.CompilerParams(dimension_semantics=("arbitrary",),
                                             vmem_limit_bytes=VMEM_LIMIT),
        name="scatter_rows",
    )(pos_flat, h2)


def _expert_kernel(te_ref, tv_ref, na_ref, x_ref, wgu_ref, bgu_ref, wdn_ref, bdn_ref, y_ref,
                   wgu_bf, wdn_bf):
    i = pl.program_id(0)
    e = te_ref[i]
    valid = tv_ref[i]
    prev_e = te_ref[jnp.maximum(i - 1, 0)]

    @pl.when((valid > 0) & ((i == 0) | (e != prev_e)))
    def _():
        wgu_bf[...] = wgu_ref[0].astype(BF16)
        wdn_bf[...] = wdn_ref[0].astype(BF16)

    @pl.when(valid > 0)
    def _():
        row = lax.broadcasted_iota(I32, (TM, 1), 0)
        x = jnp.where(row < valid, x_ref[...], 0.0).astype(BF16)
        gu = _dot(x, wgu_bf[...]) + bgu_ref[0]
        gate = jnp.minimum(gu[:, :D_FF], SWIGLU_LIMIT)
        up = jnp.clip(gu[:, D_FF:], -SWIGLU_LIMIT, SWIGLU_LIMIT)
        act = (up + 1.0) * (gate * _sigmoid(SWIGLU_ALPHA * gate))
        y_ref[...] = _dot(act.astype(BF16), wdn_bf[...]) + bdn_ref[0]


def _experts(te, tv, na, xs, w_gu, b_gu, w_dn, b_dn):
    n_tiles = xs.shape[0] // TM

    def x_map(i, te, tv, na):
        return (jnp.minimum(i, na[0] - 1), 0)

    def w_map(i, te, tv, na):
        return (te[i], 0, 0)

    return pl.pallas_call(
        _expert_kernel,
        grid_spec=pltpu.PrefetchScalarGridSpec(
            num_scalar_prefetch=3,
            grid=(n_tiles,),
            in_specs=[pl.BlockSpec((TM, D), x_map),
                      pl.BlockSpec((1, D, 2 * D_FF), w_map),
                      pl.BlockSpec((1, 1, 2 * D_FF), w_map),
                      pl.BlockSpec((1, D_FF, D), w_map),
                      pl.BlockSpec((1, 1, D), w_map)],
            out_specs=pl.BlockSpec((TM, D), x_map),
            scratch_shapes=[pltpu.VMEM((D, 2 * D_FF), BF16), pltpu.VMEM((D_FF, D), BF16)]),
        out_shape=jax.ShapeDtypeStruct(xs.shape, F32),
        compiler_params=pltpu.CompilerParams(dimension_semantics=("arbitrary",),
                                             vmem_limit_bytes=VMEM_LIMIT),
        name="expert_ffn",
    )(te, tv, na, xs, w_gu, b_gu.reshape(N_EXPERTS, 1, 2 * D_FF), w_dn,
      b_dn.reshape(N_EXPERTS, 1, D))


def _combine_kernel(pos_ref, x1_ref, w_ref, mod_ref, fg_ref, ys_ref, out_ref, rows_ref, sem,
                    *, ns, tt, tok0, n_tok):
    i = pl.program_id(0)
    rows = ns * tt
    base = tok0 + i * rows

    def row_copy(t, k):
        p = pos_ref[k * n_tok + base + t]
        return pltpu.make_async_copy(ys_ref.at[pl.ds(p, 1)], rows_ref.at[k, pl.ds(t, 1)], sem)

    def start_body(t, _):
        for k in range(TOP_K):
            row_copy(t, k).start()
        return 0

    lax.fori_loop(0, rows, start_body, 0)

    def wait_body(t, _):
        for k in range(TOP_K):
            row_copy(t, k).wait()
        return 0

    lax.fori_loop(0, rows, wait_body, 0)

    w = w_ref[...]
    moe = rows_ref[0] * w[:, 0:1]
    for k in range(1, TOP_K):
        moe = moe + rows_ref[k] * w[:, k:k + 1]
    gt2 = mod_ref[...][:, :, 5 * D:6 * D]
    x2 = x1_ref[...].reshape(ns, tt, D) + gt2 * moe.reshape(ns, tt, D)
    r = lax.rsqrt(jnp.mean(x2 * x2, axis=-1, keepdims=True) + EPS)
    out_ref[...] = (x2 * r) * fg_ref[...]


def _combine(pos_flat, x1, topw, mod3, final_g, ys, *, bsz, t_len, ns, tt, tok0):
    n_tok = x1.shape[0]
    rows = ns * tt
    n_tt = t_len // tt
    blk0 = tok0 // rows
    kern = functools.partial(_combine_kernel, ns=ns, tt=tt, tok0=tok0, n_tok=n_tok)
    return pl.pallas_call(
        kern,
        grid_spec=pltpu.PrefetchScalarGridSpec(
            num_scalar_prefetch=1,
            grid=((bsz // ns) * n_tt,),
            in_specs=[pl.BlockSpec((rows, D), lambda i, pos: (blk0 + i, 0)),
                      pl.BlockSpec((rows, 128), lambda i, pos: (blk0 + i, 0)),
                      pl.BlockSpec((ns, 1, 6 * D), lambda i, pos: (i // n_tt, 0, 0)),
                      pl.BlockSpec((1, 1, D), lambda i, pos: (0, 0, 0)),
                      pl.BlockSpec(memory_space=pl.ANY)],
            out_specs=pl.BlockSpec((ns, tt, D), lambda i, pos: (i // n_tt, i % n_tt, 0)),
            scratch_shapes=[pltpu.VMEM((TOP_K, rows, D), F32), pltpu.SemaphoreType.DMA(())]),
        out_shape=jax.ShapeDtypeStruct((bsz, t_len, D), F32),
        compiler_params=pltpu.CompilerParams(dimension_semantics=("arbitrary",),
                                             vmem_limit_bytes=VMEM_LIMIT),
        name="combine_prompt" if ns == 1 else "combine_state",
    )(pos_flat, x1, topw, mod3, final_g.reshape(1, 1, D), ys)


def _pad_heads(w, n_in):
    lead = w.shape[:-1]
    w4 = w.reshape(lead + (HEADS, n_in))
    w4 = jnp.pad(w4, [(0, 0)] * len(lead) + [(0, 0), (0, HEAD_W - n_in)])
    return w4.reshape(lead + (HEADS * HEAD_W,))


def kernel(x_prompt, x_sample, c_prompt, c_sample, state_hgrn, state_gla, ada_w, ada_b, norm1_g, norm2_g, w_in, hgrn_lb, hgrn_norm_g, gla_gate_w2, gla_gate_b, gla_norm_g, w_branch_h, w_branch_g, w_out, router_w, router_b, exp_w_gu, exp_b_gu, exp_w_dn, exp_b_dn, final_g):
    bp, tp, _ = x_prompt.shape
    bs, ts, _ = x_sample.shape
    n_prompt, n_sample = bp * tp, bs * ts
    n_tok = n_prompt + n_sample

    wi = w_in[0]
    gk_w = 4 * GLA_DK
    o_gq = 4 * 512
    o_gk = o_gq + gk_w
    o_gv = o_gk + gk_w
    o_gr = o_gv + 512
    o_gg = o_gr + GLA_RANK
    win_p = jnp.concatenate([
        wi[:, :o_gq],
        _pad_heads(wi[:, o_gq:o_gk], GLA_DK),
        _pad_heads(wi[:, o_gk:o_gv], GLA_DK),
        wi[:, o_gv:o_gr],
        jnp.pad(wi[:, o_gr:o_gg], ((0, 0), (0, 128 - GLA_RANK))),
        wi[:, o_gg:],
    ], axis=1).astype(BF16)
    w2_p = jnp.pad(_pad_heads(gla_gate_w2[0], GLA_DK), ((0, 128 - GLA_RANK), (0, 0))).astype(BF16)
    gb_p = _pad_heads(gla_gate_b[0], GLA_DK).reshape(1, BW)
    weights = [norm1_g[0].reshape(1, 1, D), win_p, hgrn_lb.astype(F32),
               hgrn_norm_g[0].reshape(1, HEAD_W), w2_p, gb_p, gla_norm_g[0].reshape(1, HEAD_W),
               w_branch_h[0].astype(BF16), w_branch_g[0].astype(BF16), w_out[0].astype(BF16),
               norm2_g[0].reshape(1, 1, D), router_w[0].T, router_b[0].reshape(N_EXPERTS, 1)]

    mod = _modulation(jnp.concatenate([c_prompt, c_sample], axis=0), ada_w[0], ada_b[0])
    mod_p = mod[:bp].reshape(bp, 1, 6 * D)
    mod_s = mod[bp:].reshape(bs, 1, 6 * D)

    x1, h2, topi, topw, hs_p, gs_p = _mixer(
        x_prompt, mod_p, weights, None, None,
        ns=1, tt=256, chunk=GROUP, row_block0=0, n_tok_total=n_tok)
    ns_s = GROUP // ts
    x1, h2, topi, topw, hs_s, gs_s = _mixer(
        x_sample, mod_s, weights, (state_hgrn, state_gla), (x1, h2, topi, topw),
        ns=ns_s, tt=ts, chunk=ts, row_block0=n_prompt // GROUP, n_tok_total=n_tok)

    n_tiles = n_tok * TOP_K // TM + N_EXPERTS
    n_tile_pad = -(-n_tiles // 128) * 128
    pos, te, tv, na = _route(topi, n_tile_pad)
    pos_flat = pos.reshape(TOP_K * n_tok)
    xs = _scatter_rows(pos_flat, h2, n_tiles * TM)
    ys = _experts(te.reshape(n_tile_pad), tv.reshape(n_tile_pad), na.reshape(128)[:1], xs,
                  exp_w_gu[0], exp_b_gu[0], exp_w_dn[0], exp_b_dn[0])
    y_p = _combine(pos_flat, x1, topw, mod_p, final_g, ys,
                   bsz=bp, t_len=tp, ns=1, tt=256, tok0=0)
    y_s = _combine(pos_flat, x1, topw, mod_s, final_g, ys,
                   bsz=bs, t_len=ts, ns=ns_s, tt=ts, tok0=n_prompt)
    return (y_p, y_s, hs_p, gs_p, hs_s, gs_s)
```

```python
import functools

import jax
import jax.numpy as jnp
from jax import lax
from jax.experimental import pallas as pl
from jax.experimental.pallas import tpu as pltpu

F32 = jnp.float32
BF16 = jnp.bfloat16
I32 = jnp.int32

D = 1024
HEADS = 4
HEAD_W = 128
GLA_DK = 64
GLA_RANK = 16
GLA_NORM = 16.0
N_EXPERTS = 32
TOP_K = 4
D_FF = 1024
SWIGLU_LIMIT = 7.0
SWIGLU_ALPHA = 1.702
EPS = 1e-6

BW = HEADS * HEAD_W
C_HQ, C_HF, C_HI, C_HG = 0, 512, 1024, 1536
GW = HEADS * GLA_DK
C_GQ, C_GK, C_GV = 2048, 2304, 2560
C_GR = 3072
C_GG = 3200
C_MA, C_MB = 3712, 4736
PW = 5760

GROUP = 128
PROMPT_TILE = 512
TM = 256
TOK_TILE = 256
ROW_UNROLL = 8
VMEM_LIMIT = 56 * 1024 * 1024


def _dot(a, b):
    return jnp.dot(a, b, preferred_element_type=F32)


def _dot_nt(a, b):
    return lax.dot_general(a, b, (((1,), (1,)), ((), ())), preferred_element_type=F32)


def _dot_tn(a, b):
    return lax.dot_general(a, b, (((0,), (0,)), ((), ())), preferred_element_type=F32)


def _split3(x):
    h1 = x.astype(BF16)
    r1 = x - h1.astype(F32)
    h2 = r1.astype(BF16)
    r2 = r1 - h2.astype(F32)
    return h1, h2, r2.astype(BF16)


def _split2(x):
    h1 = x.astype(BF16)
    return h1, (x - h1.astype(F32)).astype(BF16)


def _dot3(a, b):
    ah, al = _split2(a)
    bh, bl = _split2(b)
    return _dot(ah, bh) + _dot(al, bh) + _dot(ah, bl)


def _sigmoid(x):
    return 1.0 / (1.0 + jnp.exp(-x))


def _silu(x):
    return x * _sigmoid(x)


def _log_sigmoid(x):
    return jnp.minimum(x, 0.0) - jnp.log(1.0 + jnp.exp(-jnp.abs(x)))


def _mod_kernel(c_ref, w_ref, b_ref, o_ref):
    c = c_ref[...]
    o_ref[...] = _dot3(_silu(c), w_ref[...]) + b_ref[...]


def _modulation(c_all, ada_w, ada_b):
    n = c_all.shape[0]
    bn = 1024
    return pl.pallas_call(
        _mod_kernel,
        grid=(6 * D // bn,),
        in_specs=[pl.BlockSpec((n, D), lambda i: (0, 0)),
                  pl.BlockSpec((D, bn), lambda i: (0, i)),
                  pl.BlockSpec((1, bn), lambda i: (0, i))],
        out_specs=pl.BlockSpec((n, bn), lambda i: (0, i)),
        out_shape=jax.ShapeDtypeStruct((n, 6 * D), F32),
        compiler_params=pltpu.CompilerParams(dimension_semantics=("arbitrary",),
                                             vmem_limit_bytes=VMEM_LIMIT),
        name="adaln_mod",
    )(c_all, ada_w, ada_b.reshape(1, 6 * D))


def _recurrence(q, k, lg, v, st_ref, tri_mat, tri_mask, blk_mask, chunk, heads_per_slab):
    n = lg.shape[1]
    nsg = GROUP // chunk
    l1, l2 = _split2(lg)
    b = _dot(tri_mat, l1) + _dot(tri_mat, l2)
    if nsg == 1:
        b_last = jnp.broadcast_to(b[GROUP - 1:GROUP], b.shape)
    else:
        b3 = b.reshape(nsg, chunk, n)
        b_last = jnp.broadcast_to(b3[:, chunk - 1:chunk, :], b3.shape).reshape(GROUP, n)
    qd = (q * jnp.exp(b)).astype(BF16)
    kd = (k * jnp.exp(-b)).astype(BF16)
    kdec = (k * jnp.exp(b_last - b)).astype(BF16)
    a = jnp.exp(b_last)
    vb = v.astype(BF16)
    lane = lax.broadcasted_iota(I32, (GROUP, HEAD_W), 1)
    zero = jnp.zeros((), BF16)
    half_w = HEAD_W // heads_per_slab
    outs = []
    for h in range(HEADS):
        slab = h // heads_per_slab
        cs = slice(slab * HEAD_W, (slab + 1) * HEAD_W)
        qd_h, kd_h, kdec_h = qd[:, cs], kd[:, cs], kdec[:, cs]
        if heads_per_slab > 1:
            sub = h % heads_per_slab
            own = (lane >= sub * half_w) & (lane < (sub + 1) * half_w)
            kd_h = jnp.where(own, kd_h, zero)
            kdec_h = jnp.where(own, kdec_h, zero)
        v_h = vb[:, h * HEAD_W:(h + 1) * HEAD_W]
        att = jnp.where(tri_mask, _dot_nt(qd_h, kd_h), 0.0).astype(BF16)
        st = st_ref[h]
        if nsg == 1:
            q_blk, k_blk, a_row = qd_h, kdec_h, a[0:1, cs]
        else:
            q_blk = jnp.where(blk_mask, jnp.concatenate([qd_h] * nsg, axis=1), zero)
            k_blk = jnp.where(blk_mask, jnp.concatenate([kdec_h] * nsg, axis=1), zero)
            a_row = jnp.concatenate([a[i * chunk:i * chunk + 1, cs] for i in range(nsg)], axis=1)
        o = _dot(att, v_h) + _dot_nt(q_blk, st.astype(BF16))
        st_ref[h] = st * a_row + _dot_tn(v_h, k_blk)
        outs.append(o)
    return jnp.concatenate(outs, axis=1)


def _head_norm_gate(o, gain, gate):
    parts = []
    for h in range(HEADS):
        oh = o[:, h * HEAD_W:(h + 1) * HEAD_W]
        r = lax.rsqrt(jnp.mean(oh * oh, axis=-1, keepdims=True) + EPS)
        parts.append(oh * r * gain)
    return jnp.concatenate(parts, axis=1) * _silu(gate)


def _mixer_kernel(*refs, ns, tt, chunk, has_state, n_alias):
    (x_ref, mod_ref, n1_ref, win_ref, lb_ref, hng_ref, w2_ref, gb_ref, gng_ref,
     wbh_ref, wbg_ref, wout_ref, n2_ref, rwt_ref, rb_ref) = refs[:15]
    pos = 15
    if has_state:
        sh0_ref, sg0_ref = refs[pos:pos + 2]
        pos += 2
    pos += n_alias
    x1_ref, h2_ref, topi_ref, topw_ref, shout_ref, sgout_ref = refs[pos:pos + 6]
    z_ref, o_ref, sth_ref, stg_ref = refs[pos + 6:]

    rows = ns * tt
    n_groups = rows // GROUP
    nsg = GROUP // chunk
    j = pl.program_id(1)

    @pl.when(j == 0)
    def _():
        if has_state:
            for n in range(ns):
                for h in range(HEADS):
                    sth_ref[h, :, n * HEAD_W:(n + 1) * HEAD_W] = sh0_ref[0, n, h].T
                    pad = jnp.zeros((HEAD_W - GLA_DK, HEAD_W), F32)
                    g0 = [sg0_ref[0, n, h], pad] if h % 2 == 0 else [pad, sg0_ref[0, n, h]]
                    stg_ref[h, :, n * HEAD_W:(n + 1) * HEAD_W] = jnp.concatenate(g0, axis=0).T
        else:
            sth_ref[...] = jnp.zeros_like(sth_ref)
            stg_ref[...] = jnp.zeros_like(stg_ref)

    x = x_ref[...]
    mod = mod_ref[...]
    sh1, sc1, gt1 = mod[:, :, 0:D], mod[:, :, D:2 * D], mod[:, :, 2 * D:3 * D]
    sh2, sc2 = mod[:, :, 3 * D:4 * D], mod[:, :, 4 * D:5 * D]
    r = lax.rsqrt(jnp.mean(x * x, axis=-1, keepdims=True) + EPS)
    h = (x * r) * n1_ref[...] * (1.0 + sc1) + sh1
    hb = h.reshape(rows, D).astype(BF16)

    cw = 512
    for c0 in range(0, PW, cw):
        c1 = min(c0 + cw, PW)
        z_ref[:, c0:c1] = _dot(hb, win_ref[:, c0:c1])

    ri = lax.broadcasted_iota(I32, (GROUP, GROUP), 0)
    ci = lax.broadcasted_iota(I32, (GROUP, GROUP), 1)
    tri_mask = ((ri // chunk) == (ci // chunk)) & (ci <= ri)
    tri_mat = jnp.where(tri_mask, 1.0, 0.0).astype(BF16)
    if nsg > 1:
        rb_i = lax.broadcasted_iota(I32, (GROUP, nsg * HEAD_W), 0)
        cb_i = lax.broadcasted_iota(I32, (GROUP, nsg * HEAD_W), 1)
        blk_mask = (rb_i // chunk) == (cb_i // HEAD_W)
    else:
        blk_mask = None

    lb_raw = lb_ref[...]
    lb_e = jnp.exp(lb_raw - jnp.max(lb_raw, axis=0, keepdims=True))
    lb = lb_e[0:1] / jnp.sum(lb_e, axis=0, keepdims=True)
    for g in range(n_groups):
        rs = slice(g * GROUP, (g + 1) * GROUP)
        hq = z_ref[rs, C_HQ:C_HQ + BW]
        f = lb + (1.0 - lb) * _sigmoid(z_ref[rs, C_HF:C_HF + BW])
        o_h = _recurrence(_silu(hq) * (HEAD_W ** -0.5), 1.0 - f, jnp.log(f),
                          z_ref[rs, C_HI:C_HI + BW], sth_ref, tri_mat, tri_mask, blk_mask, chunk, 1)
        o_ref[rs, 0:BW] = o_h
        xg = _dot(z_ref[rs, C_GR:C_GR + 128].astype(BF16), w2_ref[...]) + gb_ref[...]
        o_g = _recurrence(z_ref[rs, C_GQ:C_GQ + GW] * (GLA_DK ** -0.5), z_ref[rs, C_GK:C_GK + GW],
                          _log_sigmoid(xg) / GLA_NORM, z_ref[rs, C_GV:C_GV + BW],
                          stg_ref, tri_mat, tri_mask, blk_mask, chunk, 2)
        o_ref[rs, BW:2 * BW] = o_g

    on_h = _head_norm_gate(o_ref[:, 0:BW], hng_ref[...], z_ref[:, C_HG:C_HG + BW])
    on_g = _head_norm_gate(o_ref[:, BW:2 * BW], gng_ref[...], z_ref[:, C_GG:C_GG + BW])
    merged = (_sigmoid(z_ref[:, C_MA:C_MA + D]) * _dot(on_h.astype(BF16), wbh_ref[...])
              + _sigmoid(z_ref[:, C_MB:C_MB + D]) * _dot(on_g.astype(BF16), wbg_ref[...]))
    y = _dot(merged.astype(BF16), wout_ref[...])
    x1 = x + gt1 * y.reshape(ns, tt, D)
    x1_ref[...] = x1.reshape(rows, D)

    r2 = lax.rsqrt(jnp.mean(x1 * x1, axis=-1, keepdims=True) + EPS)
    h2 = ((x1 * r2) * n2_ref[...] * (1.0 + sc2) + sh2).reshape(rows, D)
    h2_ref[...] = h2
    hh, hl = _split2(h2)
    wh, wl = _split2(rwt_ref[...])
    logits = _dot_nt(wh, hh) + _dot_nt(wl, hh) + _dot_nt(wh, hl) + rb_ref[...]
    e_iota = lax.broadcasted_iota(I32, (N_EXPERTS, rows), 0).astype(F32)
    vals, idxs = [], []
    for _ in range(TOP_K):
        m = jnp.max(logits, axis=0, keepdims=True)
        i = jnp.min(jnp.where(logits == m, e_iota, float(N_EXPERTS)), axis=0, keepdims=True)
        vals.append(m)
        idxs.append(i)
        logits = jnp.where(e_iota == i, -jnp.inf, logits)
    es = [jnp.exp(v - vals[0]) for v in vals]
    inv = 1.0 / (es[0] + es[1] + es[2] + es[3])
    topi_ref[...] = jnp.concatenate(idxs, axis=0).astype(I32)
    w_t = jnp.concatenate([e * inv for e in es] + [jnp.zeros((128 - TOP_K, rows), F32)], axis=0)
    topw_ref[...] = w_t.T

    @pl.when(j == pl.num_programs(1) - 1)
    def _():
        for n in range(ns if nsg > 1 else 1):
            for hd in range(HEADS):
                shout_ref[0, n, hd] = sth_ref[hd, :, n * HEAD_W:(n + 1) * HEAD_W].T
                k0 = (hd % 2) * GLA_DK
                sgout_ref[0, n, hd] = stg_ref[hd, :, n * HEAD_W:(n + 1) * HEAD_W].T[k0:k0 + GLA_DK]


def _const_spec(shape):
    nd = len(shape)
    return pl.BlockSpec(shape, lambda s, j: (0,) * nd, pipeline_mode=pl.Buffered(1))


def _mixer(x, mod3, weights, states, alias_bufs, *, ns, tt, chunk, row_block0, n_tok_total):
    bsz, t_len, _ = x.shape
    rows = ns * tt
    n_sb, n_tt = bsz // ns, t_len // tt
    has_state = states is not None
    nss = ns if chunk < GROUP else 1
    assert (chunk == GROUP and ns == 1) or (rows == GROUP and tt == chunk)

    def row_blk(s, j):
        return (row_block0 + s * n_tt + j, 0)

    in_specs = [pl.BlockSpec((ns, tt, D), lambda s, j: (s, j, 0)),
                pl.BlockSpec((ns, 1, 6 * D), lambda s, j: (s, 0, 0))]
    in_specs += [_const_spec(w.shape) for w in weights]
    args = [x, mod3, *weights]
    if has_state:
        in_specs += [pl.BlockSpec((1, ns, HEADS, 128, 128), lambda s, j: (0, s, 0, 0, 0)),
                     pl.BlockSpec((1, ns, HEADS, GLA_DK, 128), lambda s, j: (0, s, 0, 0, 0))]
        args += list(states)
    aliases = {}
    if alias_bufs is not None:
        for k, buf in enumerate(alias_bufs):
            in_specs.append(pl.BlockSpec(memory_space=pl.ANY))
            aliases[len(args)] = k
            args.append(buf)
    n_alias = 0 if alias_bufs is None else len(alias_bufs)

    out_shape = [jax.ShapeDtypeStruct((n_tok_total, D), F32),
                 jax.ShapeDtypeStruct((n_tok_total, D), F32),
                 jax.ShapeDtypeStruct((TOP_K, n_tok_total), I32),
                 jax.ShapeDtypeStruct((n_tok_total, 128), F32),
                 jax.ShapeDtypeStruct((1, bsz, HEADS, 128, 128), F32),
                 jax.ShapeDtypeStruct((1, bsz, HEADS, GLA_DK, 128), F32)]
    out_specs = [pl.BlockSpec((rows, D), row_blk),
                 pl.BlockSpec((rows, D), row_blk),
                 pl.BlockSpec((TOP_K, rows), lambda s, j: (0, row_block0 + s * n_tt + j)),
                 pl.BlockSpec((rows, 128), row_blk),
                 pl.BlockSpec((1, ns, HEADS, 128, 128), lambda s, j: (0, s, 0, 0, 0)),
                 pl.BlockSpec((1, ns, HEADS, GLA_DK, 128), lambda s, j: (0, s, 0, 0, 0))]
    scratch = [pltpu.VMEM((rows, PW), F32),
               pltpu.VMEM((rows, 2 * BW), F32),
               pltpu.VMEM((HEADS, HEAD_W, nss * HEAD_W), F32),
               pltpu.VMEM((HEADS, HEAD_W, nss * HEAD_W), F32)]
    kern = functools.partial(_mixer_kernel, ns=ns, tt=tt, chunk=chunk,
                             has_state=has_state, n_alias=n_alias)
    return pl.pallas_call(
        kern,
        grid=(n_sb, n_tt),
        in_specs=in_specs,
        out_specs=out_specs,
        out_shape=out_shape,
        scratch_shapes=scratch,
        input_output_aliases=aliases,
        compiler_params=pltpu.CompilerParams(dimension_semantics=("arbitrary", "arbitrary"),
                                             vmem_limit_bytes=VMEM_LIMIT),
        name="mixer_state" if has_state else "mixer_prompt",
    )(*args)


def _route_kernel(topi_ref, pos_ref, te_ref, tv_ref, na_ref, tf_ref, tn_ref, ts_ref, rank_ref,
                  *, n_tok, n_tile_pad):
    blk = 256
    nb = n_tok // blk
    e_iota = lax.broadcasted_iota(I32, (N_EXPERTS, blk), 0)
    upper = jnp.where(lax.broadcasted_iota(I32, (blk, blk), 0)
                      < lax.broadcasted_iota(I32, (blk, blk), 1), 1.0, 0.0).astype(BF16)

    def onehot(off):
        ti = topi_ref[:, pl.ds(off, blk)]
        oh = jnp.zeros((N_EXPERTS, blk), F32)
        for k in range(TOP_K):
            oh = oh + jnp.where(e_iota == ti[k:k + 1, :], 1.0, 0.0)
        return ti, oh

    def rank_body(jb, carry):
        off = pl.multiple_of(jb * blk, blk)
        _, oh = onehot(off)
        rank_ref[:, pl.ds(off, blk)] = _dot(oh.astype(BF16), upper) + carry
        return carry + jnp.sum(oh, axis=1, keepdims=True)

    counts = lax.fori_loop(0, nb, rank_body, jnp.zeros((N_EXPERTS, 1), F32))
    gsize = jnp.floor((counts + (TM - 1)) * (1.0 / TM)) * TM
    ee_r = lax.broadcasted_iota(I32, (N_EXPERTS, N_EXPERTS), 0)
    ee_c = lax.broadcasted_iota(I32, (N_EXPERTS, N_EXPERTS), 1)
    gsize_row = jnp.sum(jnp.where(ee_r == ee_c, gsize, 0.0), axis=0, keepdims=True)
    gstart = jnp.sum(jnp.where(ee_c < ee_r, gsize_row, 0.0), axis=1, keepdims=True)

    def pos_body(jb, _):
        off = pl.multiple_of(jb * blk, blk)
        ti, _unused = onehot(off)
        base = rank_ref[:, pl.ds(off, blk)] + gstart
        rows = [jnp.sum(jnp.where(e_iota == ti[k:k + 1, :], base, 0.0), axis=0, keepdims=True)
                for k in range(TOP_K)]
        pos_ref[:, pl.ds(off, blk)] = jnp.concatenate(rows, axis=0).astype(I32)
        return 0

    lax.fori_loop(0, nb, pos_body, 0)

    t0 = lax.broadcasted_iota(I32, (N_EXPERTS, n_tile_pad), 1).astype(F32) * TM
    te_iota = lax.broadcasted_iota(I32, (N_EXPERTS, n_tile_pad), 0)
    gend = gstart + gsize
    te = jnp.minimum(jnp.sum(jnp.where(gend <= t0, 1.0, 0.0), axis=0, keepdims=True),
                     N_EXPERTS - 1.0)
    sel = te_iota == te.astype(I32)
    left = jnp.sum(jnp.where(sel, counts - (t0 - gstart), 0.0), axis=0, keepdims=True)
    te_ref[...] = te.astype(I32)
    tv_ref[...] = jnp.clip(left, 0.0, float(TM)).astype(I32)
    total = jnp.sum(gsize, axis=0, keepdims=True) * (1.0 / TM)
    na_ref[...] = jnp.broadcast_to(total, (1, 128)).astype(I32)
    te_f = te_iota.astype(F32)
    nonempty = counts > 0.0
    first = jnp.sum(jnp.where(sel & (gstart == t0), 1.0, 0.0), axis=0, keepdims=True)
    nxt = jnp.min(jnp.where((te_f > te) & nonempty, te_f, float(N_EXPERTS)), axis=0, keepdims=True)
    grp = jnp.sum(jnp.where((te_f < te) & nonempty, 1.0, 0.0), axis=0, keepdims=True)
    tf_ref[...] = first.astype(I32)
    tn_ref[...] = jnp.where(nxt < float(N_EXPERTS), nxt, te).astype(I32)
    ts_ref[...] = (grp - 2.0 * jnp.floor(grp * 0.5)).astype(I32)


def _route(topi_t, n_tile_pad):
    n_tok = topi_t.shape[1]
    kern = functools.partial(_route_kernel, n_tok=n_tok, n_tile_pad=n_tile_pad)
    return pl.pallas_call(
        kern,
        out_shape=[jax.ShapeDtypeStruct((TOP_K, n_tok), I32),
                   jax.ShapeDtypeStruct((1, n_tile_pad), I32),
                   jax.ShapeDtypeStruct((1, n_tile_pad), I32),
                   jax.ShapeDtypeStruct((1, 128), I32),
                   jax.ShapeDtypeStruct((1, n_tile_pad), I32),
                   jax.ShapeDtypeStruct((1, n_tile_pad), I32),
                   jax.ShapeDtypeStruct((1, n_tile_pad), I32)],
        scratch_shapes=[pltpu.VMEM((N_EXPERTS, n_tok), F32)],
        compiler_params=pltpu.CompilerParams(vmem_limit_bytes=VMEM_LIMIT),
        name="route_plan",
    )(topi_t)


def _scatter_kernel(pos_ref, h2_ref, xs_ref, sem, *, n_tok):
    i = pl.program_id(0)
    base = i * TOK_TILE

    def row_copy(t, k):
        p = pos_ref[k * n_tok + base + t]
        return pltpu.make_async_copy(h2_ref.at[pl.ds(t, 1)], xs_ref.at[pl.ds(p, 1)], sem)

    def start_body(tb, _):
        t0 = pl.multiple_of(tb * ROW_UNROLL, ROW_UNROLL)
        for u in range(ROW_UNROLL):
            for k in range(TOP_K):
                row_copy(t0 + u, k).start(priority=(u * TOP_K + k) % 2)
        return 0

    lax.fori_loop(0, TOK_TILE // ROW_UNROLL, start_body, 0)

    def wait_body(t, _):
        for k in range(TOP_K):
            row_copy(t, k).wait()
        return 0

    lax.fori_loop(0, TOK_TILE, wait_body, 0)


def _scatter_rows(pos_flat, h2, n_rows_sorted):
    n_tok = h2.shape[0]
    kern = functools.partial(_scatter_kernel, n_tok=n_tok)
    return pl.pallas_call(
        kern,
        grid_spec=pltpu.PrefetchScalarGridSpec(
            num_scalar_prefetch=1,
            grid=(n_tok // TOK_TILE,),
            in_specs=[pl.BlockSpec((TOK_TILE, D), lambda i, pos: (i, 0))],
            out_specs=pl.BlockSpec(memory_space=pl.ANY),
            scratch_shapes=[pltpu.SemaphoreType.DMA(())]),
        out_shape=jax.ShapeDtypeStruct((n_rows_sorted, D), F32),
        compiler_params=pltpu.CompilerParams(dimension_semantics=("arbitrary",),
                                             vmem_limit_bytes=VMEM_LIMIT),
        name="scatter_rows",
    )(pos_flat, h2)


def _expert_kernel(te_ref, tv_ref, na_ref, tf_ref, tn_ref, ts_ref,
                   x_ref, bgu_ref, bdn_ref, wgu_hbm, wdn_hbm, y_ref,
                   wgu_f32, wdn_f32, wgu_bf, wdn_bf, sems):
    i = pl.program_id(0)
    e = te_ref[i]
    valid = tv_ref[i]
    slot = ts_ref[i]

    def weight_copies(expert, buf):
        return (pltpu.make_async_copy(wgu_hbm.at[expert], wgu_f32.at[buf], sems.at[0, buf]),
                pltpu.make_async_copy(wdn_hbm.at[expert], wdn_f32.at[buf], sems.at[1, buf]))

    @pl.when((valid > 0) & (tf_ref[i] == 1))
    def _():
        @pl.when(i == 0)
        def _():
            for cp in weight_copies(e, slot):
                cp.start()

        for cp in weight_copies(e, slot):
            cp.wait()
        wgu_bf[...] = wgu_f32[slot].astype(BF16)
        wdn_bf[...] = wdn_f32[slot].astype(BF16)
        nxt = tn_ref[i]

        @pl.when(nxt != e)
        def _():
            for cp in weight_copies(nxt, 1 - slot):
                cp.start()

    @pl.when(valid > 0)
    def _():
        row = lax.broadcasted_iota(I32, (TM, 1), 0)
        x = jnp.where(row < valid, x_ref[...], 0.0).astype(BF16)
        gu = _dot(x, wgu_bf[...]) + bgu_ref[0]
        gate = jnp.minimum(gu[:, :D_FF], SWIGLU_LIMIT)
        up = jnp.clip(gu[:, D_FF:], -SWIGLU_LIMIT, SWIGLU_LIMIT)
        act = (up + 1.0) * (gate * _sigmoid(SWIGLU_ALPHA * gate))
        y_ref[...] = _dot(act.astype(BF16), wdn_bf[...]) + bdn_ref[0]


def _experts(plan, xs, w_gu, b_gu, w_dn, b_dn):
    n_tiles = xs.shape[0] // TM

    def x_map(i, te, tv, na, tf, tn, ts):
        return (jnp.minimum(i, na[0] - 1), 0)

    def b_map(i, te, tv, na, tf, tn, ts):
        return (te[i], 0, 0)

    return pl.pallas_call(
        _expert_kernel,
        grid_spec=pltpu.PrefetchScalarGridSpec(
            num_scalar_prefetch=6,
            grid=(n_tiles,),
            in_specs=[pl.BlockSpec((TM, D), x_map),
                      pl.BlockSpec((1, 1, 2 * D_FF), b_map),
                      pl.BlockSpec((1, 1, D), b_map),
                      pl.BlockSpec(memory_space=pl.ANY),
                      pl.BlockSpec(memory_space=pl.ANY)],
            out_specs=pl.BlockSpec((TM, D), x_map),
            scratch_shapes=[pltpu.VMEM((2, D, 2 * D_FF), F32), pltpu.VMEM((2, D_FF, D), F32),
                            pltpu.VMEM((D, 2 * D_FF), BF16), pltpu.VMEM((D_FF, D), BF16),
                            pltpu.SemaphoreType.DMA((2, 2))]),
        out_shape=jax.ShapeDtypeStruct(xs.shape, F32),
        compiler_params=pltpu.CompilerParams(dimension_semantics=("arbitrary",),
                                             vmem_limit_bytes=VMEM_LIMIT),
        name="expert_ffn",
    )(*plan, xs, b_gu.reshape(N_EXPERTS, 1, 2 * D_FF), b_dn.reshape(N_EXPERTS, 1, D), w_gu, w_dn)


def _combine_kernel(pos_ref, x1_ref, w_ref, mod_ref, fg_ref, ys_ref, out_ref, rows_ref, sems,
                    *, ns, tt, tok0, n_tok):
    i = pl.program_id(0)
    n_steps = pl.num_programs(0)
    rows = ns * tt
    slot = i % 2

    def row_copy(step, buf, t, k):
        p = pos_ref[k * n_tok + tok0 + step * rows + t]
        return pltpu.make_async_copy(ys_ref.at[pl.ds(p, 1)], rows_ref.at[buf, k, pl.ds(t, 1)],
                                     sems.at[buf])

    def issue(step, buf):
        def body(tb, _):
            t0 = pl.multiple_of(tb * ROW_UNROLL, ROW_UNROLL)
            for u in range(ROW_UNROLL):
                for k in range(TOP_K):
                    row_copy(step, buf, t0 + u, k).start(priority=(u * TOP_K + k) % 2)
            return 0

        lax.fori_loop(0, rows // ROW_UNROLL, body, 0)

    @pl.when(i == 0)
    def _():
        issue(0, 0)

    @pl.when(i + 1 < n_steps)
    def _():
        issue(i + 1, 1 - slot)

    def wait_body(t, _):
        for k in range(TOP_K):
            row_copy(i, slot, t, k).wait()
        return 0

    lax.fori_loop(0, rows, wait_body, 0)

    w = w_ref[...]
    moe = rows_ref[slot, 0] * w[:, 0:1]
    for k in range(1, TOP_K):
        moe = moe + rows_ref[slot, k] * w[:, k:k + 1]
    gt2 = mod_ref[...][:, :, 5 * D:6 * D]
    x2 = x1_ref[...].reshape(ns, tt, D) + gt2 * moe.reshape(ns, tt, D)
    r = lax.rsqrt(jnp.mean(x2 * x2, axis=-1, keepdims=True) + EPS)
    out_ref[...] = (x2 * r) * fg_ref[...]


def _combine(pos_flat, x1, topw, mod3, final_g, ys, *, bsz, t_len, ns, tt, tok0):
    n_tok = x1.shape[0]
    rows = ns * tt
    n_tt = t_len // tt
    blk0 = tok0 // rows
    kern = functools.partial(_combine_kernel, ns=ns, tt=tt, tok0=tok0, n_tok=n_tok)
    return pl.pallas_call(
        kern,
        grid_spec=pltpu.PrefetchScalarGridSpec(
            num_scalar_prefetch=1,
            grid=((bsz // ns) * n_tt,),
            in_specs=[pl.BlockSpec((rows, D), lambda i, pos: (blk0 + i, 0)),
                      pl.BlockSpec((rows, 128), lambda i, pos: (blk0 + i, 0)),
                      pl.BlockSpec((ns, 1, 6 * D), lambda i, pos: (i // n_tt, 0, 0)),
                      pl.BlockSpec((1, 1, D), lambda i, pos: (0, 0, 0)),
                      pl.BlockSpec(memory_space=pl.ANY)],
            out_specs=pl.BlockSpec((ns, tt, D), lambda i, pos: (i // n_tt, i % n_tt, 0)),
            scratch_shapes=[pltpu.VMEM((2, TOP_K, rows, D), F32), pltpu.SemaphoreType.DMA((2,))]),
        out_shape=jax.ShapeDtypeStruct((bsz, t_len, D), F32),
        compiler_params=pltpu.CompilerParams(dimension_semantics=("arbitrary",),
                                             vmem_limit_bytes=VMEM_LIMIT),
        name="combine_prompt" if ns == 1 else "combine_state",
    )(pos_flat, x1, topw, mod3, final_g.reshape(1, 1, D), ys)


def kernel(x_prompt, x_sample, c_prompt, c_sample, state_hgrn, state_gla, ada_w, ada_b, norm1_g, norm2_g, w_in, hgrn_lb, hgrn_norm_g, gla_gate_w2, gla_gate_b, gla_norm_g, w_branch_h, w_branch_g, w_out, router_w, router_b, exp_w_gu, exp_b_gu, exp_w_dn, exp_b_dn, final_g):
    bp, tp, _ = x_prompt.shape
    bs, ts, _ = x_sample.shape
    n_prompt, n_sample = bp * tp, bs * ts
    n_tok = n_prompt + n_sample

    wi = w_in[0]
    o_gr = C_GR
    o_gg = o_gr + GLA_RANK
    win_p = jnp.concatenate([
        wi[:, :o_gr],
        jnp.pad(wi[:, o_gr:o_gg], ((0, 0), (0, 128 - GLA_RANK))),
        wi[:, o_gg:],
    ], axis=1).astype(BF16)
    w2_p = jnp.pad(gla_gate_w2[0], ((0, 128 - GLA_RANK), (0, 0))).astype(BF16)
    gb_p = gla_gate_b[0].reshape(1, GW)
    weights = [norm1_g[0].reshape(1, 1, D), win_p, hgrn_lb.astype(F32),
               hgrn_norm_g[0].reshape(1, HEAD_W), w2_p, gb_p, gla_norm_g[0].reshape(1, HEAD_W),
               w_branch_h[0].astype(BF16), w_branch_g[0].astype(BF16), w_out[0].astype(BF16),
               norm2_g[0].reshape(1, 1, D), router_w[0].T, router_b[0].reshape(N_EXPERTS, 1)]

    mod = _modulation(jnp.concatenate([c_prompt, c_sample], axis=0), ada_w[0], ada_b[0])
    mod_p = mod[:bp].reshape(bp, 1, 6 * D)
    mod_s = mod[bp:].reshape(bs, 1, 6 * D)

    x1, h2, topi, topw, hs_p, gs_p = _mixer(
        x_prompt, mod_p, weights, None, None,
        ns=1, tt=PROMPT_TILE, chunk=GROUP, row_block0=0, n_tok_total=n_tok)
    ns_s = GROUP // ts
    x1, h2, topi, topw, hs_s, gs_s = _mixer(
        x_sample, mod_s, weights, (state_hgrn, state_gla), (x1, h2, topi, topw),
        ns=ns_s, tt=ts, chunk=ts, row_block0=n_prompt // GROUP, n_tok_total=n_tok)

    n_tiles = n_tok * TOP_K // TM + N_EXPERTS
    n_tile_pad = -(-n_tiles // 128) * 128
    pos, te, tv, na, tf, tn, ts_ = _route(topi, n_tile_pad)
    pos_flat = pos.reshape(TOP_K * n_tok)
    xs = _scatter_rows(pos_flat, h2, n_tiles * TM)
    plan = (te.reshape(n_tile_pad), tv.reshape(n_tile_pad), na.reshape(128)[:1],
            tf.reshape(n_tile_pad), tn.reshape(n_tile_pad), ts_.reshape(n_tile_pad))
    ys = _experts(plan, xs, exp_w_gu[0], exp_b_gu[0], exp_w_dn[0], exp_b_dn[0])
    y_p = _combine(pos_flat, x1, topw, mod_p, final_g, ys,
                   bsz=bp, t_len=tp, ns=1, tt=256, tok0=0)
    y_s = _combine(pos_flat, x1, topw, mod_s, final_g, ys,
                   bsz=bs, t_len=ts, ns=ns_s, tt=ts, tok0=n_prompt)
    return (y_p, y_s, hs_p, gs_p, hs_s, gs_s)
```

```python
import functools

import jax
import jax.numpy as jnp
from jax import lax
from jax.experimental import pallas as pl
from jax.experimental.pallas import tpu as pltpu

F32 = jnp.float32
BF16 = jnp.bfloat16
I32 = jnp.int32

D = 1024
HEADS = 4
HEAD_W = 128
GLA_DK = 64
GLA_RANK = 16
GLA_NORM = 16.0
N_EXPERTS = 32
TOP_K = 4
D_FF = 1024
SWIGLU_LIMIT = 7.0
SWIGLU_ALPHA = 1.702
EPS = 1e-6

BW = HEADS * HEAD_W
C_HQ, C_HF, C_HI, C_HG = 0, 512, 1024, 1536
GW = HEADS * GLA_DK
C_GQ, C_GK, C_GV = 2048, 2304, 2560
C_GR = 3072
C_GG = 3200
C_MA, C_MB = 3712, 4736
PW = 5760

GROUP = 128
PROMPT_TILE = 512
TM = 256
TOK_TILE = 256
ROW_UNROLL = 8
ROW_TILES = D // 128
VMEM_LIMIT = 56 * 1024 * 1024


def _dot(a, b):
    return jnp.dot(a, b, preferred_element_type=F32)


def _dot_nt(a, b):
    return lax.dot_general(a, b, (((1,), (1,)), ((), ())), preferred_element_type=F32)


def _dot_tn(a, b):
    return lax.dot_general(a, b, (((0,), (0,)), ((), ())), preferred_element_type=F32)


def _split3(x):
    h1 = x.astype(BF16)
    r1 = x - h1.astype(F32)
    h2 = r1.astype(BF16)
    r2 = r1 - h2.astype(F32)
    return h1, h2, r2.astype(BF16)


def _split2(x):
    h1 = x.astype(BF16)
    return h1, (x - h1.astype(F32)).astype(BF16)


def _dot3(a, b):
    ah, al = _split2(a)
    bh, bl = _split2(b)
    return _dot(ah, bh) + _dot(al, bh) + _dot(ah, bl)


def _sigmoid(x):
    return 1.0 / (1.0 + jnp.exp(-x))


def _silu(x):
    return x * _sigmoid(x)


def _log_sigmoid(x):
    return jnp.minimum(x, 0.0) - jnp.log(1.0 + jnp.exp(-jnp.abs(x)))


def _store_token_rows(ref, mat, n, lead=()):
    for c in range(ROW_TILES):
        ref[lead + (pl.ds(c, n, stride=ROW_TILES), slice(None))] = mat[:, c * 128:(c + 1) * 128]


def _load_token_rows(ref, n, lead=()):
    return jnp.concatenate(
        [ref[lead + (pl.ds(c, n, stride=ROW_TILES), slice(None))] for c in range(ROW_TILES)], axis=1)


def _mod_kernel(c_ref, w_ref, b_ref, o_ref):
    c = c_ref[...]
    o_ref[...] = _dot3(_silu(c), w_ref[...]) + b_ref[...]


def _modulation(c_all, ada_w, ada_b):
    n = c_all.shape[0]
    bn = 1024
    return pl.pallas_call(
        _mod_kernel,
        grid=(6 * D // bn,),
        in_specs=[pl.BlockSpec((n, D), lambda i: (0, 0)),
                  pl.BlockSpec((D, bn), lambda i: (0, i)),
                  pl.BlockSpec((1, bn), lambda i: (0, i))],
        out_specs=pl.BlockSpec((n, bn), lambda i: (0, i)),
        out_shape=jax.ShapeDtypeStruct((n, 6 * D), F32),
        compiler_params=pltpu.CompilerParams(dimension_semantics=("arbitrary",),
                                             vmem_limit_bytes=VMEM_LIMIT),
        name="adaln_mod",
    )(c_all, ada_w, ada_b.reshape(1, 6 * D))


def _recurrence(q, k, lg, v, st_ref, tri_mat, tri_mask, blk_mask, chunk, heads_per_slab):
    n = lg.shape[1]
    nsg = GROUP // chunk
    l1, l2 = _split2(lg)
    b = _dot(tri_mat, l1) + _dot(tri_mat, l2)
    if nsg == 1:
        b_last = jnp.broadcast_to(b[GROUP - 1:GROUP], b.shape)
    else:
        b3 = b.reshape(nsg, chunk, n)
        b_last = jnp.broadcast_to(b3[:, chunk - 1:chunk, :], b3.shape).reshape(GROUP, n)
    qd = (q * jnp.exp(b)).astype(BF16)
    kd = (k * jnp.exp(-b)).astype(BF16)
    kdec = (k * jnp.exp(b_last - b)).astype(BF16)
    a = jnp.exp(b_last)
    vb = v.astype(BF16)
    lane = lax.broadcasted_iota(I32, (GROUP, HEAD_W), 1)
    zero = jnp.zeros((), BF16)
    half_w = HEAD_W // heads_per_slab
    outs = []
    for h in range(HEADS):
        slab = h // heads_per_slab
        cs = slice(slab * HEAD_W, (slab + 1) * HEAD_W)
        qd_h, kd_h, kdec_h = qd[:, cs], kd[:, cs], kdec[:, cs]
        if heads_per_slab > 1:
            sub = h % heads_per_slab
            own = (lane >= sub * half_w) & (lane < (sub + 1) * half_w)
            kd_h = jnp.where(own, kd_h, zero)
            kdec_h = jnp.where(own, kdec_h, zero)
        v_h = vb[:, h * HEAD_W:(h + 1) * HEAD_W]
        att = jnp.where(tri_mask, _dot_nt(qd_h, kd_h), 0.0).astype(BF16)
        st = st_ref[h]
        if nsg == 1:
            q_blk, k_blk, a_row = qd_h, kdec_h, a[0:1, cs]
        else:
            q_blk = jnp.where(blk_mask, jnp.concatenate([qd_h] * nsg, axis=1), zero)
            k_blk = jnp.where(blk_mask, jnp.concatenate([kdec_h] * nsg, axis=1), zero)
            a_row = jnp.concatenate([a[i * chunk:i * chunk + 1, cs] for i in range(nsg)], axis=1)
        o = _dot(att, v_h) + _dot_nt(q_blk, st.astype(BF16))
        st_ref[h] = st * a_row + _dot_tn(v_h, k_blk)
        outs.append(o)
    return jnp.concatenate(outs, axis=1)


def _head_norm_gate(o, gain, gate):
    parts = []
    for h in range(HEADS):
        oh = o[:, h * HEAD_W:(h + 1) * HEAD_W]
        r = lax.rsqrt(jnp.mean(oh * oh, axis=-1, keepdims=True) + EPS)
        parts.append(oh * r * gain)
    return jnp.concatenate(parts, axis=1) * _silu(gate)


def _mixer_kernel(*refs, ns, tt, chunk, has_state, n_alias):
    (x_ref, mod_ref, n1_ref, win_ref, lb_ref, hng_ref, w2_ref, gb_ref, gng_ref,
     wbh_ref, wbg_ref, wout_ref, n2_ref, rwt_ref, rb_ref) = refs[:15]
    pos = 15
    if has_state:
        sh0_ref, sg0_ref = refs[pos:pos + 2]
        pos += 2
    pos += n_alias
    x1_ref, h2_ref, topi_ref, topw_ref, shout_ref, sgout_ref = refs[pos:pos + 6]
    z_ref, o_ref, sth_ref, stg_ref = refs[pos + 6:]

    rows = ns * tt
    n_groups = rows // GROUP
    nsg = GROUP // chunk
    j = pl.program_id(1)

    @pl.when(j == 0)
    def _():
        if has_state:
            for n in range(ns):
                for h in range(HEADS):
                    sth_ref[h, :, n * HEAD_W:(n + 1) * HEAD_W] = sh0_ref[0, n, h].T
                    pad = jnp.zeros((HEAD_W - GLA_DK, HEAD_W), F32)
                    g0 = [sg0_ref[0, n, h], pad] if h % 2 == 0 else [pad, sg0_ref[0, n, h]]
                    stg_ref[h, :, n * HEAD_W:(n + 1) * HEAD_W] = jnp.concatenate(g0, axis=0).T
        else:
            sth_ref[...] = jnp.zeros_like(sth_ref)
            stg_ref[...] = jnp.zeros_like(stg_ref)

    x = x_ref[...]
    mod = mod_ref[...]
    sh1, sc1, gt1 = mod[:, :, 0:D], mod[:, :, D:2 * D], mod[:, :, 2 * D:3 * D]
    sh2, sc2 = mod[:, :, 3 * D:4 * D], mod[:, :, 4 * D:5 * D]
    r = lax.rsqrt(jnp.mean(x * x, axis=-1, keepdims=True) + EPS)
    h = (x * r) * n1_ref[...] * (1.0 + sc1) + sh1
    hb = h.reshape(rows, D).astype(BF16)

    cw = 512
    for c0 in range(0, PW, cw):
        c1 = min(c0 + cw, PW)
        z_ref[:, c0:c1] = _dot(hb, win_ref[:, c0:c1])

    ri = lax.broadcasted_iota(I32, (GROUP, GROUP), 0)
    ci = lax.broadcasted_iota(I32, (GROUP, GROUP), 1)
    tri_mask = ((ri // chunk) == (ci // chunk)) & (ci <= ri)
    tri_mat = jnp.where(tri_mask, 1.0, 0.0).astype(BF16)
    if nsg > 1:
        rb_i = lax.broadcasted_iota(I32, (GROUP, nsg * HEAD_W), 0)
        cb_i = lax.broadcasted_iota(I32, (GROUP, nsg * HEAD_W), 1)
        blk_mask = (rb_i // chunk) == (cb_i // HEAD_W)
    else:
        blk_mask = None

    lb_raw = lb_ref[...]
    lb_e = jnp.exp(lb_raw - jnp.max(lb_raw, axis=0, keepdims=True))
    lb = lb_e[0:1] / jnp.sum(lb_e, axis=0, keepdims=True)
    for g in range(n_groups):
        rs = slice(g * GROUP, (g + 1) * GROUP)
        hq = z_ref[rs, C_HQ:C_HQ + BW]
        f = lb + (1.0 - lb) * _sigmoid(z_ref[rs, C_HF:C_HF + BW])
        o_h = _recurrence(_silu(hq) * (HEAD_W ** -0.5), 1.0 - f, jnp.log(f),
                          z_ref[rs, C_HI:C_HI + BW], sth_ref, tri_mat, tri_mask, blk_mask, chunk, 1)
        o_ref[rs, 0:BW] = o_h
        xg = _dot(z_ref[rs, C_GR:C_GR + 128].astype(BF16), w2_ref[...]) + gb_ref[...]
        o_g = _recurrence(z_ref[rs, C_GQ:C_GQ + GW] * (GLA_DK ** -0.5), z_ref[rs, C_GK:C_GK + GW],
                          _log_sigmoid(xg) / GLA_NORM, z_ref[rs, C_GV:C_GV + BW],
                          stg_ref, tri_mat, tri_mask, blk_mask, chunk, 2)
        o_ref[rs, BW:2 * BW] = o_g

    on_h = _head_norm_gate(o_ref[:, 0:BW], hng_ref[...], z_ref[:, C_HG:C_HG + BW])
    on_g = _head_norm_gate(o_ref[:, BW:2 * BW], gng_ref[...], z_ref[:, C_GG:C_GG + BW])
    merged = (_sigmoid(z_ref[:, C_MA:C_MA + D]) * _dot(on_h.astype(BF16), wbh_ref[...])
              + _sigmoid(z_ref[:, C_MB:C_MB + D]) * _dot(on_g.astype(BF16), wbg_ref[...]))
    y = _dot(merged.astype(BF16), wout_ref[...])
    x1 = x + gt1 * y.reshape(ns, tt, D)
    x1_ref[...] = x1.reshape(rows, D)

    r2 = lax.rsqrt(jnp.mean(x1 * x1, axis=-1, keepdims=True) + EPS)
    h2 = ((x1 * r2) * n2_ref[...] * (1.0 + sc2) + sh2).reshape(rows, D)
    _store_token_rows(h2_ref, h2, rows)
    hh, hl = _split2(h2)
    wh, wl = _split2(rwt_ref[...])
    logits = _dot_nt(wh, hh) + _dot_nt(wl, hh) + _dot_nt(wh, hl) + rb_ref[...]
    e_iota = lax.broadcasted_iota(I32, (N_EXPERTS, rows), 0).astype(F32)
    vals, idxs = [], []
    for _ in range(TOP_K):
        m = jnp.max(logits, axis=0, keepdims=True)
        i = jnp.min(jnp.where(logits == m, e_iota, float(N_EXPERTS)), axis=0, keepdims=True)
        vals.append(m)
        idxs.append(i)
        logits = jnp.where(e_iota == i, -jnp.inf, logits)
    es = [jnp.exp(v - vals[0]) for v in vals]
    inv = 1.0 / (es[0] + es[1] + es[2] + es[3])
    topi_ref[...] = jnp.concatenate(idxs, axis=0).astype(I32)
    w_t = jnp.concatenate([e * inv for e in es] + [jnp.zeros((128 - TOP_K, rows), F32)], axis=0)
    topw_ref[...] = w_t.T

    @pl.when(j == pl.num_programs(1) - 1)
    def _():
        for n in range(ns if nsg > 1 else 1):
            for hd in range(HEADS):
                shout_ref[0, n, hd] = sth_ref[hd, :, n * HEAD_W:(n + 1) * HEAD_W].T
                k0 = (hd % 2) * GLA_DK
                sgout_ref[0, n, hd] = stg_ref[hd, :, n * HEAD_W:(n + 1) * HEAD_W].T[k0:k0 + GLA_DK]


def _const_spec(shape):
    nd = len(shape)
    return pl.BlockSpec(shape, lambda s, j: (0,) * nd, pipeline_mode=pl.Buffered(1))


def _mixer(x, mod3, weights, states, alias_bufs, *, ns, tt, chunk, row_block0, n_tok_total):
    bsz, t_len, _ = x.shape
    rows = ns * tt
    n_sb, n_tt = bsz // ns, t_len // tt
    has_state = states is not None
    nss = ns if chunk < GROUP else 1
    assert (chunk == GROUP and ns == 1) or (rows == GROUP and tt == chunk)

    def row_blk(s, j):
        return (row_block0 + s * n_tt + j, 0)

    in_specs = [pl.BlockSpec((ns, tt, D), lambda s, j: (s, j, 0)),
                pl.BlockSpec((ns, 1, 6 * D), lambda s, j: (s, 0, 0))]
    in_specs += [_const_spec(w.shape) for w in weights]
    args = [x, mod3, *weights]
    if has_state:
        in_specs += [pl.BlockSpec((1, ns, HEADS, 128, 128), lambda s, j: (0, s, 0, 0, 0)),
                     pl.BlockSpec((1, ns, HEADS, GLA_DK, 128), lambda s, j: (0, s, 0, 0, 0))]
        args += list(states)
    aliases = {}
    if alias_bufs is not None:
        for k, buf in enumerate(alias_bufs):
            in_specs.append(pl.BlockSpec(memory_space=pl.ANY))
            aliases[len(args)] = k
            args.append(buf)
    n_alias = 0 if alias_bufs is None else len(alias_bufs)

    out_shape = [jax.ShapeDtypeStruct((n_tok_total, D), F32),
                 jax.ShapeDtypeStruct((n_tok_total * ROW_TILES, 128), F32),
                 jax.ShapeDtypeStruct((TOP_K, n_tok_total), I32),
                 jax.ShapeDtypeStruct((n_tok_total, 128), F32),
                 jax.ShapeDtypeStruct((1, bsz, HEADS, 128, 128), F32),
                 jax.ShapeDtypeStruct((1, bsz, HEADS, GLA_DK, 128), F32)]
    out_specs = [pl.BlockSpec((rows, D), row_blk),
                 pl.BlockSpec((rows * ROW_TILES, 128), row_blk),
                 pl.BlockSpec((TOP_K, rows), lambda s, j: (0, row_block0 + s * n_tt + j)),
                 pl.BlockSpec((rows, 128), row_blk),
                 pl.BlockSpec((1, ns, HEADS, 128, 128), lambda s, j: (0, s, 0, 0, 0)),
                 pl.BlockSpec((1, ns, HEADS, GLA_DK, 128), lambda s, j: (0, s, 0, 0, 0))]
    scratch = [pltpu.VMEM((rows, PW), F32),
               pltpu.VMEM((rows, 2 * BW), F32),
               pltpu.VMEM((HEADS, HEAD_W, nss * HEAD_W), F32),
               pltpu.VMEM((HEADS, HEAD_W, nss * HEAD_W), F32)]
    kern = functools.partial(_mixer_kernel, ns=ns, tt=tt, chunk=chunk,
                             has_state=has_state, n_alias=n_alias)
    return pl.pallas_call(
        kern,
        grid=(n_sb, n_tt),
        in_specs=in_specs,
        out_specs=out_specs,
        out_shape=out_shape,
        scratch_shapes=scratch,
        input_output_aliases=aliases,
        compiler_params=pltpu.CompilerParams(dimension_semantics=("arbitrary", "arbitrary"),
                                             vmem_limit_bytes=VMEM_LIMIT),
        name="mixer_state" if has_state else "mixer_prompt",
    )(*args)


def _route_kernel(topi_ref, pos_ref, te_ref, tv_ref, na_ref, tf_ref, tn_ref, ts_ref, rank_ref,
                  *, n_tok, n_tile_pad):
    blk = 256
    nb = n_tok // blk
    e_iota = lax.broadcasted_iota(I32, (N_EXPERTS, blk), 0)
    upper = jnp.where(lax.broadcasted_iota(I32, (blk, blk), 0)
                      < lax.broadcasted_iota(I32, (blk, blk), 1), 1.0, 0.0).astype(BF16)

    def onehot(off):
        ti = topi_ref[:, pl.ds(off, blk)]
        oh = jnp.zeros((N_EXPERTS, blk), F32)
        for k in range(TOP_K):
            oh = oh + jnp.where(e_iota == ti[k:k + 1, :], 1.0, 0.0)
        return ti, oh

    def rank_body(jb, carry):
        off = pl.multiple_of(jb * blk, blk)
        _, oh = onehot(off)
        rank_ref[:, pl.ds(off, blk)] = _dot(oh.astype(BF16), upper) + carry
        return carry + jnp.sum(oh, axis=1, keepdims=True)

    counts = lax.fori_loop(0, nb, rank_body, jnp.zeros((N_EXPERTS, 1), F32))
    gsize = jnp.floor((counts + (TM - 1)) * (1.0 / TM)) * TM
    ee_r = lax.broadcasted_iota(I32, (N_EXPERTS, N_EXPERTS), 0)
    ee_c = lax.broadcasted_iota(I32, (N_EXPERTS, N_EXPERTS), 1)
    gsize_row = jnp.sum(jnp.where(ee_r == ee_c, gsize, 0.0), axis=0, keepdims=True)
    gstart = jnp.sum(jnp.where(ee_c < ee_r, gsize_row, 0.0), axis=1, keepdims=True)

    def pos_body(jb, _):
        off = pl.multiple_of(jb * blk, blk)
        ti, _unused = onehot(off)
        base = rank_ref[:, pl.ds(off, blk)] + gstart
        rows = [jnp.sum(jnp.where(e_iota == ti[k:k + 1, :], base, 0.0), axis=0, keepdims=True)
                for k in range(TOP_K)]
        pos_ref[:, pl.ds(off, blk)] = jnp.concatenate(rows, axis=0).astype(I32)
        return 0

    lax.fori_loop(0, nb, pos_body, 0)

    t0 = lax.broadcasted_iota(I32, (N_EXPERTS, n_tile_pad), 1).astype(F32) * TM
    te_iota = lax.broadcasted_iota(I32, (N_EXPERTS, n_tile_pad), 0)
    gend = gstart + gsize
    te = jnp.minimum(jnp.sum(jnp.where(gend <= t0, 1.0, 0.0), axis=0, keepdims=True),
                     N_EXPERTS - 1.0)
    sel = te_iota == te.astype(I32)
    left = jnp.sum(jnp.where(sel, counts - (t0 - gstart), 0.0), axis=0, keepdims=True)
    te_ref[...] = te.astype(I32)
    tv_ref[...] = jnp.clip(left, 0.0, float(TM)).astype(I32)
    total = jnp.sum(gsize, axis=0, keepdims=True) * (1.0 / TM)
    na_ref[...] = jnp.broadcast_to(total, (1, 128)).astype(I32)
    te_f = te_iota.astype(F32)
    nonempty = counts > 0.0
    first = jnp.sum(jnp.where(sel & (gstart == t0), 1.0, 0.0), axis=0, keepdims=True)
    nxt = jnp.min(jnp.where((te_f > te) & nonempty, te_f, float(N_EXPERTS)), axis=0, keepdims=True)
    grp = jnp.sum(jnp.where((te_f < te) & nonempty, 1.0, 0.0), axis=0, keepdims=True)
    tf_ref[...] = first.astype(I32)
    tn_ref[...] = jnp.where(nxt < float(N_EXPERTS), nxt, te).astype(I32)
    ts_ref[...] = (grp - 2.0 * jnp.floor(grp * 0.5)).astype(I32)


def _route(topi_t, n_tile_pad):
    n_tok = topi_t.shape[1]
    kern = functools.partial(_route_kernel, n_tok=n_tok, n_tile_pad=n_tile_pad)
    return pl.pallas_call(
        kern,
        out_shape=[jax.ShapeDtypeStruct((TOP_K, n_tok), I32),
                   jax.ShapeDtypeStruct((1, n_tile_pad), I32),
                   jax.ShapeDtypeStruct((1, n_tile_pad), I32),
                   jax.ShapeDtypeStruct((1, 128), I32),
                   jax.ShapeDtypeStruct((1, n_tile_pad), I32),
                   jax.ShapeDtypeStruct((1, n_tile_pad), I32),
                   jax.ShapeDtypeStruct((1, n_tile_pad), I32)],
        scratch_shapes=[pltpu.VMEM((N_EXPERTS, n_tok), F32)],
        compiler_params=pltpu.CompilerParams(vmem_limit_bytes=VMEM_LIMIT),
        name="route_plan",
    )(topi_t)


def _scatter_kernel(pos_ref, h2_ref, xs_ref, sem, *, n_tok):
    i = pl.program_id(0)
    base = i * TOK_TILE

    def row_copy(t, k):
        p = pos_ref[k * n_tok + base + t]
        src = h2_ref.at[pl.ds(pl.multiple_of(t * ROW_TILES, ROW_TILES), ROW_TILES)]
        dst = xs_ref.at[pl.ds(pl.multiple_of(p * ROW_TILES, ROW_TILES), ROW_TILES)]
        return pltpu.make_async_copy(src, dst, sem)

    def start_body(tb, _):
        t0 = tb * ROW_UNROLL
        for u in range(ROW_UNROLL):
            for k in range(TOP_K):
                row_copy(t0 + u, k).start(priority=(u * TOP_K + k) % 2)
        return 0

    lax.fori_loop(0, TOK_TILE // ROW_UNROLL, start_body, 0)

    def wait_body(t, _):
        for k in range(TOP_K):
            row_copy(t, k).wait()
        return 0

    lax.fori_loop(0, TOK_TILE, wait_body, 0)


def _scatter_rows(pos_flat, h2, n_rows_sorted):
    n_tok = h2.shape[0] // ROW_TILES
    kern = functools.partial(_scatter_kernel, n_tok=n_tok)
    return pl.pallas_call(
        kern,
        grid_spec=pltpu.PrefetchScalarGridSpec(
            num_scalar_prefetch=1,
            grid=(n_tok // TOK_TILE,),
            in_specs=[pl.BlockSpec((TOK_TILE * ROW_TILES, 128), lambda i, pos: (i, 0))],
            out_specs=pl.BlockSpec(memory_space=pl.ANY),
            scratch_shapes=[pltpu.SemaphoreType.DMA(())]),
        out_shape=jax.ShapeDtypeStruct((n_rows_sorted * ROW_TILES, 128), F32),
        compiler_params=pltpu.CompilerParams(dimension_semantics=("arbitrary",),
                                             vmem_limit_bytes=VMEM_LIMIT),
        name="scatter_rows",
    )(pos_flat, h2)


def _expert_kernel(te_ref, tv_ref, na_ref, tf_ref, tn_ref, ts_ref,
                   x_ref, bgu_ref, bdn_ref, wgu_hbm, wdn_hbm, y_ref,
                   wgu_f32, wdn_f32, wgu_bf, wdn_bf, sems):
    i = pl.program_id(0)
    e = te_ref[i]
    valid = tv_ref[i]
    slot = ts_ref[i]

    def weight_copies(expert, buf):
        return (pltpu.make_async_copy(wgu_hbm.at[expert], wgu_f32.at[buf], sems.at[0, buf]),
                pltpu.make_async_copy(wdn_hbm.at[expert], wdn_f32.at[buf], sems.at[1, buf]))

    @pl.when((valid > 0) & (tf_ref[i] == 1))
    def _():
        @pl.when(i == 0)
        def _():
            for cp in weight_copies(e, slot):
                cp.start()

        for cp in weight_copies(e, slot):
            cp.wait()
        wgu_bf[...] = wgu_f32[slot].astype(BF16)
        wdn_bf[...] = wdn_f32[slot].astype(BF16)
        nxt = tn_ref[i]

        @pl.when(nxt != e)
        def _():
            for cp in weight_copies(nxt, 1 - slot):
                cp.start()

    @pl.when(valid > 0)
    def _():
        row = lax.broadcasted_iota(I32, (TM, 1), 0)
        x = jnp.where(row < valid, _load_token_rows(x_ref, TM), 0.0).astype(BF16)
        gu = _dot(x, wgu_bf[...]) + bgu_ref[0]
        gate = jnp.minimum(gu[:, :D_FF], SWIGLU_LIMIT)
        up = jnp.clip(gu[:, D_FF:], -SWIGLU_LIMIT, SWIGLU_LIMIT)
        act = (up + 1.0) * (gate * _sigmoid(SWIGLU_ALPHA * gate))
        _store_token_rows(y_ref, _dot(act.astype(BF16), wdn_bf[...]) + bdn_ref[0], TM)


def _experts(plan, xs, w_gu, b_gu, w_dn, b_dn):
    n_tiles = xs.shape[0] // (TM * ROW_TILES)

    def x_map(i, te, tv, na, tf, tn, ts):
        return (jnp.minimum(i, na[0] - 1), 0)

    def b_map(i, te, tv, na, tf, tn, ts):
        return (te[i], 0, 0)

    return pl.pallas_call(
        _expert_kernel,
        grid_spec=pltpu.PrefetchScalarGridSpec(
            num_scalar_prefetch=6,
            grid=(n_tiles,),
            in_specs=[pl.BlockSpec((TM * ROW_TILES, 128), x_map),
                      pl.BlockSpec((1, 1, 2 * D_FF), b_map),
                      pl.BlockSpec((1, 1, D), b_map),
                      pl.BlockSpec(memory_space=pl.ANY),
                      pl.BlockSpec(memory_space=pl.ANY)],
            out_specs=pl.BlockSpec((TM * ROW_TILES, 128), x_map),
            scratch_shapes=[pltpu.VMEM((2, D, 2 * D_FF), F32), pltpu.VMEM((2, D_FF, D), F32),
                            pltpu.VMEM((D, 2 * D_FF), BF16), pltpu.VMEM((D_FF, D), BF16),
                            pltpu.SemaphoreType.DMA((2, 2))]),
        out_shape=jax.ShapeDtypeStruct(xs.shape, F32),
        compiler_params=pltpu.CompilerParams(dimension_semantics=("arbitrary",),
                                             vmem_limit_bytes=VMEM_LIMIT),
        name="expert_ffn",
    )(*plan, xs, b_gu.reshape(N_EXPERTS, 1, 2 * D_FF), b_dn.reshape(N_EXPERTS, 1, D), w_gu, w_dn)


def _combine_kernel(pos_ref, x1_ref, w_ref, mod_ref, fg_ref, ys_ref, out_ref, rows_ref, sems,
                    *, ns, tt, tok0, n_tok):
    i = pl.program_id(0)
    n_steps = pl.num_programs(0)
    rows = ns * tt
    slot = i % 2

    def row_copy(step, buf, t, k):
        p = pos_ref[k * n_tok + tok0 + step * rows + t]
        src = ys_ref.at[pl.ds(pl.multiple_of(p * ROW_TILES, ROW_TILES), ROW_TILES)]
        dst = rows_ref.at[buf, k, pl.ds(pl.multiple_of(t * ROW_TILES, ROW_TILES), ROW_TILES)]
        return pltpu.make_async_copy(src, dst, sems.at[buf])

    def issue(step, buf):
        def body(tb, _):
            t0 = tb * ROW_UNROLL
            for u in range(ROW_UNROLL):
                for k in range(TOP_K):
                    row_copy(step, buf, t0 + u, k).start(priority=(u * TOP_K + k) % 2)
            return 0

        lax.fori_loop(0, rows // ROW_UNROLL, body, 0)

    @pl.when(i == 0)
    def _():
        issue(0, 0)

    @pl.when(i + 1 < n_steps)
    def _():
        issue(i + 1, 1 - slot)

    def wait_body(t, _):
        for k in range(TOP_K):
            row_copy(i, slot, t, k).wait()
        return 0

    lax.fori_loop(0, rows, wait_body, 0)

    w = w_ref[...]
    moe = _load_token_rows(rows_ref, rows, (slot, 0)) * w[:, 0:1]
    for k in range(1, TOP_K):
        moe = moe + _load_token_rows(rows_ref, rows, (slot, k)) * w[:, k:k + 1]
    gt2 = mod_ref[...][:, :, 5 * D:6 * D]
    x2 = x1_ref[...].reshape(ns, tt, D) + gt2 * moe.reshape(ns, tt, D)
    r = lax.rsqrt(jnp.mean(x2 * x2, axis=-1, keepdims=True) + EPS)
    out_ref[...] = (x2 * r) * fg_ref[...]


def _combine(pos_flat, x1, topw, mod3, final_g, ys, *, bsz, t_len, ns, tt, tok0):
    n_tok = x1.shape[0]
    rows = ns * tt
    n_tt = t_len // tt
    blk0 = tok0 // rows
    kern = functools.partial(_combine_kernel, ns=ns, tt=tt, tok0=tok0, n_tok=n_tok)
    return pl.pallas_call(
        kern,
        grid_spec=pltpu.PrefetchScalarGridSpec(
            num_scalar_prefetch=1,
            grid=((bsz // ns) * n_tt,),
            in_specs=[pl.BlockSpec((rows, D), lambda i, pos: (blk0 + i, 0)),
                      pl.BlockSpec((rows, 128), lambda i, pos: (blk0 + i, 0)),
                      pl.BlockSpec((ns, 1, 6 * D), lambda i, pos: (i // n_tt, 0, 0)),
                      pl.BlockSpec((1, 1, D), lambda i, pos: (0, 0, 0)),
                      pl.BlockSpec(memory_space=pl.ANY)],
            out_specs=pl.BlockSpec((ns, tt, D), lambda i, pos: (i // n_tt, i % n_tt, 0)),
            scratch_shapes=[pltpu.VMEM((2, TOP_K, rows * ROW_TILES, 128), F32),
                            pltpu.SemaphoreType.DMA((2,))]),
        out_shape=jax.ShapeDtypeStruct((bsz, t_len, D), F32),
        compiler_params=pltpu.CompilerParams(dimension_semantics=("arbitrary",),
                                             vmem_limit_bytes=VMEM_LIMIT),
        name="combine_prompt" if ns == 1 else "combine_state",
    )(pos_flat, x1, topw, mod3, final_g.reshape(1, 1, D), ys)


def kernel(x_prompt, x_sample, c_prompt, c_sample, state_hgrn, state_gla, ada_w, ada_b, norm1_g, norm2_g, w_in, hgrn_lb, hgrn_norm_g, gla_gate_w2, gla_gate_b, gla_norm_g, w_branch_h, w_branch_g, w_out, router_w, router_b, exp_w_gu, exp_b_gu, exp_w_dn, exp_b_dn, final_g):
    bp, tp, _ = x_prompt.shape
    bs, ts, _ = x_sample.shape
    n_prompt, n_sample = bp * tp, bs * ts
    n_tok = n_prompt + n_sample

    wi = w_in[0]
    o_gr = C_GR
    o_gg = o_gr + GLA_RANK
    win_p = jnp.concatenate([
        wi[:, :o_gr],
        jnp.pad(wi[:, o_gr:o_gg], ((0, 0), (0, 128 - GLA_RANK))),
        wi[:, o_gg:],
    ], axis=1).astype(BF16)
    w2_p = jnp.pad(gla_gate_w2[0], ((0, 128 - GLA_RANK), (0, 0))).astype(BF16)
    gb_p = gla_gate_b[0].reshape(1, GW)
    weights = [norm1_g[0].reshape(1, 1, D), win_p, hgrn_lb.astype(F32),
               hgrn_norm_g[0].reshape(1, HEAD_W), w2_p, gb_p, gla_norm_g[0].reshape(1, HEAD_W),
               w_branch_h[0].astype(BF16), w_branch_g[0].astype(BF16), w_out[0].astype(BF16),
               norm2_g[0].reshape(1, 1, D), router_w[0].T, router_b[0].reshape(N_EXPERTS, 1)]

    mod = _modulation(jnp.concatenate([c_prompt, c_sample], axis=0), ada_w[0], ada_b[0])
    mod_p = mod[:bp].reshape(bp, 1, 6 * D)
    mod_s = mod[bp:].reshape(bs, 1, 6 * D)

    x1, h2, topi, topw, hs_p, gs_p = _mixer(
        x_prompt, mod_p, weights, None, None,
        ns=1, tt=PROMPT_TILE, chunk=GROUP, row_block0=0, n_tok_total=n_tok)
    ns_s = GROUP // ts
    x1, h2, topi, topw, hs_s, gs_s = _mixer(
        x_sample, mod_s, weights, (state_hgrn, state_gla), (x1, h2, topi, topw),
        ns=ns_s, tt=ts, chunk=ts, row_block0=n_prompt // GROUP, n_tok_total=n_tok)

    n_tiles = n_tok * TOP_K // TM + N_EXPERTS
    n_tile_pad = -(-n_tiles // 128) * 128
    pos, te, tv, na, tf, tn, ts_ = _route(topi, n_tile_pad)
    pos_flat = pos.reshape(TOP_K * n_tok)
    xs = _scatter_rows(pos_flat, h2, n_tiles * TM)
    plan = (te.reshape(n_tile_pad), tv.reshape(n_tile_pad), na.reshape(128)[:1],
            tf.reshape(n_tile_pad), tn.reshape(n_tile_pad), ts_.reshape(n_tile_pad))
    ys = _experts(plan, xs, exp_w_gu[0], exp_b_gu[0], exp_w_dn[0], exp_b_dn[0])
    y_p = _combine(pos_flat, x1, topw, mod_p, final_g, ys,
                   bsz=bp, t_len=tp, ns=1, tt=256, tok0=0)
    y_s = _combine(pos_flat, x1, topw, mod_s, final_g, ys,
                   bsz=bs, t_len=ts, ns=ns_s, tt=ts, tok0=n_prompt)
    return (y_p, y_s, hs_p, gs_p, hs_s, gs_s)
```

```python
import functools

import jax
import jax.numpy as jnp
from jax import lax
from jax.experimental import pallas as pl
from jax.experimental.pallas import tpu as pltpu

F32 = jnp.float32
BF16 = jnp.bfloat16
I32 = jnp.int32

D = 1024
HEADS = 4
HEAD_W = 128
GLA_DK = 64
GLA_RANK = 16
GLA_NORM = 16.0
N_EXPERTS = 32
TOP_K = 4
D_FF = 1024
SWIGLU_LIMIT = 7.0
SWIGLU_ALPHA = 1.702
EPS = 1e-6

BW = HEADS * HEAD_W
C_HQ, C_HF, C_HI, C_HG = 0, 512, 1024, 1536
GW = HEADS * GLA_DK
QKW = BW + GW
DECAY_FLOOR = -80.0
C_GQ, C_GK, C_GV = 2048, 2304, 2560
C_GR = 3072
C_GG = 3200
C_MA, C_MB = 3712, 4736
PW = 5760

GROUP = 128
PROMPT_TILE = 512
TM = 256
TOK_TILE = 256
ROW_UNROLL = 8
ROW_TILES = D // 128
VMEM_LIMIT = 56 * 1024 * 1024


def _dot(a, b):
    return jnp.dot(a, b, preferred_element_type=F32)


def _dot_nt(a, b):
    return lax.dot_general(a, b, (((1,), (1,)), ((), ())), preferred_element_type=F32)


def _dot_tn(a, b):
    return lax.dot_general(a, b, (((0,), (0,)), ((), ())), preferred_element_type=F32)


def _split3(x):
    h1 = x.astype(BF16)
    r1 = x - h1.astype(F32)
    h2 = r1.astype(BF16)
    r2 = r1 - h2.astype(F32)
    return h1, h2, r2.astype(BF16)


def _split2(x):
    h1 = x.astype(BF16)
    return h1, (x - h1.astype(F32)).astype(BF16)


def _dot3(a, b):
    ah, al = _split2(a)
    bh, bl = _split2(b)
    return _dot(ah, bh) + _dot(al, bh) + _dot(ah, bl)


def _sigmoid(x):
    return 1.0 / (1.0 + jnp.exp(-x))


def _silu(x):
    return x * _sigmoid(x)


def _log_sigmoid(x):
    return jnp.minimum(x, 0.0) - jnp.log(1.0 + jnp.exp(-jnp.abs(x)))


def _store_token_rows(ref, mat, n, lead=()):
    for c in range(ROW_TILES):
        ref[lead + (pl.ds(c, n, stride=ROW_TILES), slice(None))] = mat[:, c * 128:(c + 1) * 128]


def _load_token_rows(ref, n, lead=()):
    return jnp.concatenate(
        [ref[lead + (pl.ds(c, n, stride=ROW_TILES), slice(None))] for c in range(ROW_TILES)], axis=1)


def _mod_kernel(c_ref, w_ref, b_ref, o_ref):
    c = c_ref[...]
    o_ref[...] = _dot3(_silu(c), w_ref[...]) + b_ref[...]


def _modulation(c_all, ada_w, ada_b):
    n = c_all.shape[0]
    bn = 1024
    return pl.pallas_call(
        _mod_kernel,
        grid=(6 * D // bn,),
        in_specs=[pl.BlockSpec((n, D), lambda i: (0, 0)),
                  pl.BlockSpec((D, bn), lambda i: (0, i)),
                  pl.BlockSpec((1, bn), lambda i: (0, i))],
        out_specs=pl.BlockSpec((n, bn), lambda i: (0, i)),
        out_shape=jax.ShapeDtypeStruct((n, 6 * D), F32),
        compiler_params=pltpu.CompilerParams(dimension_semantics=("arbitrary",),
                                             vmem_limit_bytes=VMEM_LIMIT),
        name="adaln_mod",
    )(c_all, ada_w, ada_b.reshape(1, 6 * D))


def _head_cols(hh):
    if hh < HEADS:
        return slice(hh * HEAD_W, (hh + 1) * HEAD_W), None
    h = hh - HEADS
    lane = lax.broadcasted_iota(I32, (GROUP, HEAD_W), 1)
    own = (lane >= (h % 2) * GLA_DK) & (lane < (h % 2 + 1) * GLA_DK)
    c0 = BW + (h // 2) * HEAD_W
    return slice(c0, c0 + HEAD_W), own


def _own_lanes(x, own):
    return x if own is None else jnp.where(own, x, jnp.zeros((), x.dtype))


def _att_pairwise(q_ref, k_ref, b_ref, att32_ref, att_ref, tri_mask, n_groups):
    col_id = lax.broadcasted_iota(I32, (GROUP, GROUP), 1)
    for g in range(n_groups):
        r0 = g * GROUP
        att32_ref[...] = jnp.zeros_like(att32_ref)

        def body(s, _, r0=r0):
            qg = q_ref[r0:r0 + GROUP, :]
            bg = b_ref[r0:r0 + GROUP, :]
            w = qg * jnp.exp(jnp.minimum(bg - b_ref[pl.ds(r0 + s, 1), :], 0.0)) \
                * k_ref[pl.ds(r0 + s, 1), :]
            for hh in range(2 * HEADS):
                cs, own = _head_cols(hh)
                col = jnp.sum(_own_lanes(w[:, cs], own), axis=1, keepdims=True)
                att32_ref[hh] = jnp.where(col_id == s, col, att32_ref[hh])
            return 0

        lax.fori_loop(0, GROUP, body, 0)
        for hh in range(2 * HEADS):
            att_ref[g * 2 * HEADS + hh] = jnp.where(tri_mask, att32_ref[hh], 0.0).astype(BF16)


def _head_norm_gate(o, gain, gate):
    parts = []
    for h in range(HEADS):
        oh = o[:, h * HEAD_W:(h + 1) * HEAD_W]
        r = lax.rsqrt(jnp.mean(oh * oh, axis=-1, keepdims=True) + EPS)
        parts.append(oh * r * gain)
    return jnp.concatenate(parts, axis=1) * _silu(gate)


def _mixer_kernel(*refs, ns, tt, chunk, has_state, n_alias):
    (x_ref, mod_ref, n1_ref, win_ref, lb_ref, hng_ref, w2_ref, gb_ref, gng_ref,
     wbh_ref, wbg_ref, wout_ref, n2_ref, rwt_ref, rb_ref) = refs[:15]
    pos = 15
    if has_state:
        sh0_ref, sg0_ref = refs[pos:pos + 2]
        pos += 2
    pos += n_alias
    x1_ref, h2_ref, topi_ref, topw_ref, shout_ref, sgout_ref = refs[pos:pos + 6]
    (z_ref, o_ref, sth_ref, stg_ref, q_ref, k_ref, b_ref, a_ref, qd_ref, kd_ref, kdec_ref,
     att_ref, att32_ref) = refs[pos + 6:]

    rows = ns * tt
    n_groups = rows // GROUP
    nsg = GROUP // chunk
    j = pl.program_id(1)

    @pl.when(j == 0)
    def _():
        if has_state:
            for n in range(ns):
                for h in range(HEADS):
                    sth_ref[h, :, n * HEAD_W:(n + 1) * HEAD_W] = sh0_ref[0, n, h].T
                    pad = jnp.zeros((HEAD_W - GLA_DK, HEAD_W), F32)
                    g0 = [sg0_ref[0, n, h], pad] if h % 2 == 0 else [pad, sg0_ref[0, n, h]]
                    stg_ref[h, :, n * HEAD_W:(n + 1) * HEAD_W] = jnp.concatenate(g0, axis=0).T
        else:
            sth_ref[...] = jnp.zeros_like(sth_ref)
            stg_ref[...] = jnp.zeros_like(stg_ref)

    x = x_ref[...]
    mod = mod_ref[...]
    sh1, sc1, gt1 = mod[:, :, 0:D], mod[:, :, D:2 * D], mod[:, :, 2 * D:3 * D]
    sh2, sc2 = mod[:, :, 3 * D:4 * D], mod[:, :, 4 * D:5 * D]
    r = lax.rsqrt(jnp.mean(x * x, axis=-1, keepdims=True) + EPS)
    h = (x * r) * n1_ref[...] * (1.0 + sc1) + sh1
    hb = h.reshape(rows, D).astype(BF16)

    cw = 512
    for c0 in range(0, PW, cw):
        c1 = min(c0 + cw, PW)
        z_ref[:, c0:c1] = _dot(hb, win_ref[:, c0:c1])

    ri = lax.broadcasted_iota(I32, (GROUP, GROUP), 0)
    ci = lax.broadcasted_iota(I32, (GROUP, GROUP), 1)
    tri_mask = ((ri // chunk) == (ci // chunk)) & (ci <= ri)
    tri_mat = jnp.where(tri_mask, 1.0, 0.0).astype(BF16)
    if nsg > 1:
        rb_i = lax.broadcasted_iota(I32, (GROUP, nsg * HEAD_W), 0)
        cb_i = lax.broadcasted_iota(I32, (GROUP, nsg * HEAD_W), 1)
        blk_mask = (rb_i // chunk) == (cb_i // HEAD_W)
    else:
        blk_mask = None

    lb_raw = lb_ref[...]
    lb_e = jnp.exp(lb_raw - jnp.max(lb_raw, axis=0, keepdims=True))
    lb = lb_e[0:1] / jnp.sum(lb_e, axis=0, keepdims=True)
    b_min = None
    for g in range(n_groups):
        rs = slice(g * GROUP, (g + 1) * GROUP)
        hq = z_ref[rs, C_HQ:C_HQ + BW]
        f = lb + (1.0 - lb) * _sigmoid(z_ref[rs, C_HF:C_HF + BW])
        xg = _dot(z_ref[rs, C_GR:C_GR + 128].astype(BF16), w2_ref[...]) + gb_ref[...]
        q = jnp.concatenate([_silu(hq) * (HEAD_W ** -0.5),
                             z_ref[rs, C_GQ:C_GQ + GW] * (GLA_DK ** -0.5)], axis=1)
        k = jnp.concatenate([1.0 - f, z_ref[rs, C_GK:C_GK + GW]], axis=1)
        lg = jnp.concatenate([jnp.log(f), _log_sigmoid(xg) / GLA_NORM], axis=1)
        l1, l2 = _split2(lg)
        b = _dot(tri_mat, l1) + _dot(tri_mat, l2)
        if nsg == 1:
            b_tot = b[GROUP - 1:GROUP]
            b_last = jnp.broadcast_to(b_tot, b.shape)
        else:
            b3 = b.reshape(nsg, chunk, QKW)
            b_tot = b3[:, chunk - 1, :]
            b_last = jnp.broadcast_to(b3[:, chunk - 1:chunk, :], b3.shape).reshape(GROUP, QKW)
        q_ref[rs, :] = q
        k_ref[rs, :] = k
        b_ref[rs, :] = b
        qd_ref[rs, :] = (q * jnp.exp(b)).astype(BF16)
        kd_ref[rs, :] = (k * jnp.exp(-b)).astype(BF16)
        kdec_ref[rs, :] = (k * jnp.exp(b_last - b)).astype(BF16)
        a_ref[g * nsg:(g + 1) * nsg, :] = jnp.exp(b_tot)
        g_min = jnp.min(b_tot, axis=0, keepdims=True)
        b_min = g_min if b_min is None else jnp.minimum(b_min, g_min)

    factorised_ok = jnp.min(b_min) >= DECAY_FLOOR

    @pl.when(factorised_ok)
    def _():
        for g in range(n_groups):
            rs = slice(g * GROUP, (g + 1) * GROUP)
            for hh in range(2 * HEADS):
                cs, own = _head_cols(hh)
                att = _dot_nt(qd_ref[rs, cs], _own_lanes(kd_ref[rs, cs], own))
                att_ref[g * 2 * HEADS + hh] = jnp.where(tri_mask, att, 0.0).astype(BF16)

    @pl.when(jnp.logical_not(factorised_ok))
    def _():
        _att_pairwise(q_ref, k_ref, b_ref, att32_ref, att_ref, tri_mask, n_groups)

    zero = jnp.zeros((), BF16)
    for g in range(n_groups):
        r0 = g * GROUP
        rs = slice(r0, r0 + GROUP)
        for branch, st_ref, v_col in ((0, sth_ref, C_HI), (1, stg_ref, C_GV)):
            vb = z_ref[rs, v_col:v_col + BW].astype(BF16)
            outs = []
            for h in range(HEADS):
                hh = branch * HEADS + h
                cs, own = _head_cols(hh)
                qd_h = qd_ref[rs, cs]
                kdec_h = _own_lanes(kdec_ref[rs, cs], own)
                v_h = vb[:, h * HEAD_W:(h + 1) * HEAD_W]
                st = st_ref[h]
                if nsg == 1:
                    q_blk, k_blk, a_row = qd_h, kdec_h, a_ref[g:g + 1, cs]
                else:
                    q_blk = jnp.where(blk_mask, jnp.concatenate([qd_h] * nsg, axis=1), zero)
                    k_blk = jnp.where(blk_mask, jnp.concatenate([kdec_h] * nsg, axis=1), zero)
                    a_row = jnp.concatenate(
                        [a_ref[g * nsg + i:g * nsg + i + 1, cs] for i in range(nsg)], axis=1)
                outs.append(_dot(att_ref[g * 2 * HEADS + hh], v_h) + _dot_nt(q_blk, st.astype(BF16)))
                st_ref[h] = st * a_row + _dot_tn(v_h, k_blk)
            o_ref[rs, branch * BW:(branch + 1) * BW] = jnp.concatenate(outs, axis=1)

    on_h = _head_norm_gate(o_ref[:, 0:BW], hng_ref[...], z_ref[:, C_HG:C_HG + BW])
    on_g = _head_norm_gate(o_ref[:, BW:2 * BW], gng_ref[...], z_ref[:, C_GG:C_GG + BW])
    merged = (_sigmoid(z_ref[:, C_MA:C_MA + D]) * _dot(on_h.astype(BF16), wbh_ref[...])
              + _sigmoid(z_ref[:, C_MB:C_MB + D]) * _dot(on_g.astype(BF16), wbg_ref[...]))
    y = _dot(merged.astype(BF16), wout_ref[...])
    x1 = x + gt1 * y.reshape(ns, tt, D)
    x1_ref[...] = x1.reshape(rows, D)

    r2 = lax.rsqrt(jnp.mean(x1 * x1, axis=-1, keepdims=True) + EPS)
    h2 = ((x1 * r2) * n2_ref[...] * (1.0 + sc2) + sh2).reshape(rows, D)
    _store_token_rows(h2_ref, h2, rows)
    hh, hl = _split2(h2)
    wh, wl = _split2(rwt_ref[...])
    logits = _dot_nt(wh, hh) + _dot_nt(wl, hh) + _dot_nt(wh, hl) + rb_ref[...]
    e_iota = lax.broadcasted_iota(I32, (N_EXPERTS, rows), 0).astype(F32)
    vals, idxs = [], []
    for _ in range(TOP_K):
        m = jnp.max(logits, axis=0, keepdims=True)
        i = jnp.min(jnp.where(logits == m, e_iota, float(N_EXPERTS)), axis=0, keepdims=True)
        vals.append(m)
        idxs.append(i)
        logits = jnp.where(e_iota == i, -jnp.inf, logits)
    es = [jnp.exp(v - vals[0]) for v in vals]
    inv = 1.0 / (es[0] + es[1] + es[2] + es[3])
    topi_ref[...] = jnp.concatenate(idxs, axis=0).astype(I32)
    w_t = jnp.concatenate([e * inv for e in es] + [jnp.zeros((128 - TOP_K, rows), F32)], axis=0)
    topw_ref[...] = w_t.T

    @pl.when(j == pl.num_programs(1) - 1)
    def _():
        for n in range(ns if nsg > 1 else 1):
            for hd in range(HEADS):
                shout_ref[0, n, hd] = sth_ref[hd, :, n * HEAD_W:(n + 1) * HEAD_W].T
                k0 = (hd % 2) * GLA_DK
                sgout_ref[0, n, hd] = stg_ref[hd, :, n * HEAD_W:(n + 1) * HEAD_W].T[k0:k0 + GLA_DK]


def _const_spec(shape):
    nd = len(shape)
    return pl.BlockSpec(shape, lambda s, j: (0,) * nd, pipeline_mode=pl.Buffered(1))


def _mixer(x, mod3, weights, states, alias_bufs, *, ns, tt, chunk, row_block0, n_tok_total):
    bsz, t_len, _ = x.shape
    rows = ns * tt
    n_sb, n_tt = bsz // ns, t_len // tt
    has_state = states is not None
    nss = ns if chunk < GROUP else 1
    assert (chunk == GROUP and ns == 1) or (rows == GROUP and tt == chunk)

    def row_blk(s, j):
        return (row_block0 + s * n_tt + j, 0)

    in_specs = [pl.BlockSpec((ns, tt, D), lambda s, j: (s, j, 0)),
                pl.BlockSpec((ns, 1, 6 * D), lambda s, j: (s, 0, 0))]
    in_specs += [_const_spec(w.shape) for w in weights]
    args = [x, mod3, *weights]
    if has_state:
        in_specs += [pl.BlockSpec((1, ns, HEADS, 128, 128), lambda s, j: (0, s, 0, 0, 0),
                                  pipeline_mode=pl.Buffered(1)),
                     pl.BlockSpec((1, ns, HEADS, GLA_DK, 128), lambda s, j: (0, s, 0, 0, 0),
                                  pipeline_mode=pl.Buffered(1))]
        args += list(states)
    aliases = {}
    if alias_bufs is not None:
        for k, buf in enumerate(alias_bufs):
            in_specs.append(pl.BlockSpec(memory_space=pl.ANY))
            aliases[len(args)] = k
            args.append(buf)
    n_alias = 0 if alias_bufs is None else len(alias_bufs)

    out_shape = [jax.ShapeDtypeStruct((n_tok_total, D), F32),
                 jax.ShapeDtypeStruct((n_tok_total * ROW_TILES, 128), F32),
                 jax.ShapeDtypeStruct((TOP_K, n_tok_total), I32),
                 jax.ShapeDtypeStruct((n_tok_total, 128), F32),
                 jax.ShapeDtypeStruct((1, bsz, HEADS, 128, 128), F32),
                 jax.ShapeDtypeStruct((1, bsz, HEADS, GLA_DK, 128), F32)]
    out_specs = [pl.BlockSpec((rows, D), row_blk),
                 pl.BlockSpec((rows * ROW_TILES, 128), row_blk),
                 pl.BlockSpec((TOP_K, rows), lambda s, j: (0, row_block0 + s * n_tt + j)),
                 pl.BlockSpec((rows, 128), row_blk),
                 pl.BlockSpec((1, ns, HEADS, 128, 128), lambda s, j: (0, s, 0, 0, 0)),
                 pl.BlockSpec((1, ns, HEADS, GLA_DK, 128), lambda s, j: (0, s, 0, 0, 0))]
    n_att = (rows // GROUP) * 2 * HEADS
    scratch = [pltpu.VMEM((rows, PW), F32),
               pltpu.VMEM((rows, 2 * BW), F32),
               pltpu.VMEM((HEADS, HEAD_W, nss * HEAD_W), F32),
               pltpu.VMEM((HEADS, HEAD_W, nss * HEAD_W), F32),
               pltpu.VMEM((rows, QKW), F32),
               pltpu.VMEM((rows, QKW), F32),
               pltpu.VMEM((rows, QKW), F32),
               pltpu.VMEM((max(rows // chunk, 8), QKW), F32),
               pltpu.VMEM((rows, QKW), BF16),
               pltpu.VMEM((rows, QKW), BF16),
               pltpu.VMEM((rows, QKW), BF16),
               pltpu.VMEM((n_att, GROUP, GROUP), BF16),
               pltpu.VMEM((2 * HEADS, GROUP, GROUP), F32)]
    kern = functools.partial(_mixer_kernel, ns=ns, tt=tt, chunk=chunk,
                             has_state=has_state, n_alias=n_alias)
    return pl.pallas_call(
        kern,
        grid=(n_sb, n_tt),
        in_specs=in_specs,
        out_specs=out_specs,
        out_shape=out_shape,
        scratch_shapes=scratch,
        input_output_aliases=aliases,
        compiler_params=pltpu.CompilerParams(dimension_semantics=("arbitrary", "arbitrary"),
                                             vmem_limit_bytes=VMEM_LIMIT),
        name="mixer_state" if has_state else "mixer_prompt",
    )(*args)


def _route_kernel(topi_ref, pos_ref, te_ref, tv_ref, na_ref, tf_ref, tn_ref, ts_ref, rank_ref,
                  *, n_tok, n_tile_pad):
    blk = 256
    nb = n_tok // blk
    e_iota = lax.broadcasted_iota(I32, (N_EXPERTS, blk), 0)
    upper = jnp.where(lax.broadcasted_iota(I32, (blk, blk), 0)
                      < lax.broadcasted_iota(I32, (blk, blk), 1), 1.0, 0.0).astype(BF16)

    def onehot(off):
        ti = topi_ref[:, pl.ds(off, blk)]
        oh = jnp.zeros((N_EXPERTS, blk), F32)
        for k in range(TOP_K):
            oh = oh + jnp.where(e_iota == ti[k:k + 1, :], 1.0, 0.0)
        return ti, oh

    def rank_body(jb, carry):
        off = pl.multiple_of(jb * blk, blk)
        _, oh = onehot(off)
        rank_ref[:, pl.ds(off, blk)] = _dot(oh.astype(BF16), upper) + carry
        return carry + jnp.sum(oh, axis=1, keepdims=True)

    counts = lax.fori_loop(0, nb, rank_body, jnp.zeros((N_EXPERTS, 1), F32))
    gsize = jnp.floor((counts + (TM - 1)) * (1.0 / TM)) * TM
    ee_r = lax.broadcasted_iota(I32, (N_EXPERTS, N_EXPERTS), 0)
    ee_c = lax.broadcasted_iota(I32, (N_EXPERTS, N_EXPERTS), 1)
    gsize_row = jnp.sum(jnp.where(ee_r == ee_c, gsize, 0.0), axis=0, keepdims=True)
    gstart = jnp.sum(jnp.where(ee_c < ee_r, gsize_row, 0.0), axis=1, keepdims=True)

    def pos_body(jb, _):
        off = pl.multiple_of(jb * blk, blk)
        ti, _unused = onehot(off)
        base = rank_ref[:, pl.ds(off, blk)] + gstart
        rows = [jnp.sum(jnp.where(e_iota == ti[k:k + 1, :], base, 0.0), axis=0, keepdims=True)
                for k in range(TOP_K)]
        pos_ref[:, pl.ds(off, blk)] = jnp.concatenate(rows, axis=0).astype(I32)
        return 0

    lax.fori_loop(0, nb, pos_body, 0)

    t0 = lax.broadcasted_iota(I32, (N_EXPERTS, n_tile_pad), 1).astype(F32) * TM
    te_iota = lax.broadcasted_iota(I32, (N_EXPERTS, n_tile_pad), 0)
    gend = gstart + gsize
    te = jnp.minimum(jnp.sum(jnp.where(gend <= t0, 1.0, 0.0), axis=0, keepdims=True),
                     N_EXPERTS - 1.0)
    sel = te_iota == te.astype(I32)
    left = jnp.sum(jnp.where(sel, counts - (t0 - gstart), 0.0), axis=0, keepdims=True)
    te_ref[...] = te.astype(I32)
    tv_ref[...] = jnp.clip(left, 0.0, float(TM)).astype(I32)
    total = jnp.sum(gsize, axis=0, keepdims=True) * (1.0 / TM)
    na_ref[...] = jnp.broadcast_to(total, (1, 128)).astype(I32)
    te_f = te_iota.astype(F32)
    nonempty = counts > 0.0
    first = jnp.sum(jnp.where(sel & (gstart == t0), 1.0, 0.0), axis=0, keepdims=True)
    nxt = jnp.min(jnp.where((te_f > te) & nonempty, te_f, float(N_EXPERTS)), axis=0, keepdims=True)
    grp = jnp.sum(jnp.where((te_f < te) & nonempty, 1.0, 0.0), axis=0, keepdims=True)
    tf_ref[...] = first.astype(I32)
    tn_ref[...] = jnp.where(nxt < float(N_EXPERTS), nxt, te).astype(I32)
    ts_ref[...] = (grp - 2.0 * jnp.floor(grp * 0.5)).astype(I32)


def _route(topi_t, n_tile_pad):
    n_tok = topi_t.shape[1]
    kern = functools.partial(_route_kernel, n_tok=n_tok, n_tile_pad=n_tile_pad)
    return pl.pallas_call(
        kern,
        out_shape=[jax.ShapeDtypeStruct((TOP_K, n_tok), I32),
                   jax.ShapeDtypeStruct((1, n_tile_pad), I32),
                   jax.ShapeDtypeStruct((1, n_tile_pad), I32),
                   jax.ShapeDtypeStruct((1, 128), I32),
                   jax.ShapeDtypeStruct((1, n_tile_pad), I32),
                   jax.ShapeDtypeStruct((1, n_tile_pad), I32),
                   jax.ShapeDtypeStruct((1, n_tile_pad), I32)],
        scratch_shapes=[pltpu.VMEM((N_EXPERTS, n_tok), F32)],
        compiler_params=pltpu.CompilerParams(vmem_limit_bytes=VMEM_LIMIT),
        name="route_plan",
    )(topi_t)


def _scatter_kernel(pos_ref, h2_ref, xs_ref, sem, *, n_tok):
    i = pl.program_id(0)
    base = i * TOK_TILE

    def row_copy(t, k):
        p = pos_ref[k * n_tok + base + t]
        src = h2_ref.at[pl.ds(pl.multiple_of(t * ROW_TILES, ROW_TILES), ROW_TILES)]
        dst = xs_ref.at[pl.ds(pl.multiple_of(p * ROW_TILES, ROW_TILES), ROW_TILES)]
        return pltpu.make_async_copy(src, dst, sem)

    def start_body(tb, _):
        t0 = tb * ROW_UNROLL
        for u in range(ROW_UNROLL):
            for k in range(TOP_K):
                row_copy(t0 + u, k).start(priority=(u * TOP_K + k) % 2)
        return 0

    lax.fori_loop(0, TOK_TILE // ROW_UNROLL, start_body, 0)

    def wait_body(t, _):
        for k in range(TOP_K):
            row_copy(t, k).wait()
        return 0

    lax.fori_loop(0, TOK_TILE, wait_body, 0)


def _scatter_rows(pos_flat, h2, n_rows_sorted):
    n_tok = h2.shape[0] // ROW_TILES
    kern = functools.partial(_scatter_kernel, n_tok=n_tok)
    return pl.pallas_call(
        kern,
        grid_spec=pltpu.PrefetchScalarGridSpec(
            num_scalar_prefetch=1,
            grid=(n_tok // TOK_TILE,),
            in_specs=[pl.BlockSpec((TOK_TILE * ROW_TILES, 128), lambda i, pos: (i, 0))],
            out_specs=pl.BlockSpec(memory_space=pl.ANY),
            scratch_shapes=[pltpu.SemaphoreType.DMA(())]),
        out_shape=jax.ShapeDtypeStruct((n_rows_sorted * ROW_TILES, 128), F32),
        compiler_params=pltpu.CompilerParams(dimension_semantics=("arbitrary",),
                                             vmem_limit_bytes=VMEM_LIMIT),
        name="scatter_rows",
    )(pos_flat, h2)


def _expert_kernel(te_ref, tv_ref, na_ref, tf_ref, tn_ref, ts_ref,
                   x_ref, bgu_ref, bdn_ref, wgu_hbm, wdn_hbm, y_ref,
                   wgu_f32, wdn_f32, wgu_bf, wdn_bf, sems):
    i = pl.program_id(0)
    e = te_ref[i]
    valid = tv_ref[i]
    slot = ts_ref[i]

    def weight_copies(expert, buf):
        return (pltpu.make_async_copy(wgu_hbm.at[expert], wgu_f32.at[buf], sems.at[0, buf]),
                pltpu.make_async_copy(wdn_hbm.at[expert], wdn_f32.at[buf], sems.at[1, buf]))

    @pl.when((valid > 0) & (tf_ref[i] == 1))
    def _():
        @pl.when(i == 0)
        def _():
            for cp in weight_copies(e, slot):
                cp.start()

        for cp in weight_copies(e, slot):
            cp.wait()
        wgu_bf[...] = wgu_f32[slot].astype(BF16)
        wdn_bf[...] = wdn_f32[slot].astype(BF16)
        nxt = tn_ref[i]

        @pl.when(nxt != e)
        def _():
            for cp in weight_copies(nxt, 1 - slot):
                cp.start()

    @pl.when(valid > 0)
    def _():
        row = lax.broadcasted_iota(I32, (TM, 1), 0)
        x = jnp.where(row < valid, _load_token_rows(x_ref, TM), 0.0).astype(BF16)
        gu = _dot(x, wgu_bf[...]) + bgu_ref[0]
        gate = jnp.minimum(gu[:, :D_FF], SWIGLU_LIMIT)
        up = jnp.clip(gu[:, D_FF:], -SWIGLU_LIMIT, SWIGLU_LIMIT)
        act = (up + 1.0) * (gate * _sigmoid(SWIGLU_ALPHA * gate))
        _store_token_rows(y_ref, _dot(act.astype(BF16), wdn_bf[...]) + bdn_ref[0], TM)


def _experts(plan, xs, w_gu, b_gu, w_dn, b_dn):
    n_tiles = xs.shape[0] // (TM * ROW_TILES)

    def x_map(i, te, tv, na, tf, tn, ts):
        return (jnp.minimum(i, na[0] - 1), 0)

    def b_map(i, te, tv, na, tf, tn, ts):
        return (te[i], 0, 0)

    return pl.pallas_call(
        _expert_kernel,
        grid_spec=pltpu.PrefetchScalarGridSpec(
            num_scalar_prefetch=6,
            grid=(n_tiles,),
            in_specs=[pl.BlockSpec((TM * ROW_TILES, 128), x_map),
                      pl.BlockSpec((1, 1, 2 * D_FF), b_map),
                      pl.BlockSpec((1, 1, D), b_map),
                      pl.BlockSpec(memory_space=pl.ANY),
                      pl.BlockSpec(memory_space=pl.ANY)],
            out_specs=pl.BlockSpec((TM * ROW_TILES, 128), x_map),
            scratch_shapes=[pltpu.VMEM((2, D, 2 * D_FF), F32), pltpu.VMEM((2, D_FF, D), F32),
                            pltpu.VMEM((D, 2 * D_FF), BF16), pltpu.VMEM((D_FF, D), BF16),
                            pltpu.SemaphoreType.DMA((2, 2))]),
        out_shape=jax.ShapeDtypeStruct(xs.shape, F32),
        compiler_params=pltpu.CompilerParams(dimension_semantics=("arbitrary",),
                                             vmem_limit_bytes=VMEM_LIMIT),
        name="expert_ffn",
    )(*plan, xs, b_gu.reshape(N_EXPERTS, 1, 2 * D_FF), b_dn.reshape(N_EXPERTS, 1, D), w_gu, w_dn)


def _combine_kernel(pos_ref, x1_ref, w_ref, mod_ref, fg_ref, ys_ref, out_ref, rows_ref, sems,
                    *, ns, tt, tok0, n_tok):
    i = pl.program_id(0)
    n_steps = pl.num_programs(0)
    rows = ns * tt
    slot = i % 2

    def row_copy(step, buf, t, k):
        p = pos_ref[k * n_tok + tok0 + step * rows + t]
        src = ys_ref.at[pl.ds(pl.multiple_of(p * ROW_TILES, ROW_TILES), ROW_TILES)]
        dst = rows_ref.at[buf, k, pl.ds(pl.multiple_of(t * ROW_TILES, ROW_TILES), ROW_TILES)]
        return pltpu.make_async_copy(src, dst, sems.at[buf])

    def issue(step, buf):
        def body(tb, _):
            t0 = tb * ROW_UNROLL
            for u in range(ROW_UNROLL):
                for k in range(TOP_K):
                    row_copy(step, buf, t0 + u, k).start(priority=(u * TOP_K + k) % 2)
            return 0

        lax.fori_loop(0, rows // ROW_UNROLL, body, 0)

    @pl.when(i == 0)
    def _():
        issue(0, 0)

    @pl.when(i + 1 < n_steps)
    def _():
        issue(i + 1, 1 - slot)

    def wait_body(t, _):
        for k in range(TOP_K):
            row_copy(i, slot, t, k).wait()
        return 0

    lax.fori_loop(0, rows, wait_body, 0)

    w = w_ref[...]
    moe = _load_token_rows(rows_ref, rows, (slot, 0)) * w[:, 0:1]
    for k in range(1, TOP_K):
        moe = moe + _load_token_rows(rows_ref, rows, (slot, k)) * w[:, k:k + 1]
    gt2 = mod_ref[...][:, :, 5 * D:6 * D]
    x2 = x1_ref[...].reshape(ns, tt, D) + gt2 * moe.reshape(ns, tt, D)
    r = lax.rsqrt(jnp.mean(x2 * x2, axis=-1, keepdims=True) + EPS)
    out_ref[...] = (x2 * r) * fg_ref[...]


def _combine(pos_flat, x1, topw, mod3, final_g, ys, *, bsz, t_len, ns, tt, tok0):
    n_tok = x1.shape[0]
    rows = ns * tt
    n_tt = t_len // tt
    blk0 = tok0 // rows
    kern = functools.partial(_combine_kernel, ns=ns, tt=tt, tok0=tok0, n_tok=n_tok)
    return pl.pallas_call(
        kern,
        grid_spec=pltpu.PrefetchScalarGridSpec(
            num_scalar_prefetch=1,
            grid=((bsz // ns) * n_tt,),
            in_specs=[pl.BlockSpec((rows, D), lambda i, pos: (blk0 + i, 0)),
                      pl.BlockSpec((rows, 128), lambda i, pos: (blk0 + i, 0)),
                      pl.BlockSpec((ns, 1, 6 * D), lambda i, pos: (i // n_tt, 0, 0)),
                      pl.BlockSpec((1, 1, D), lambda i, pos: (0, 0, 0)),
                      pl.BlockSpec(memory_space=pl.ANY)],
            out_specs=pl.BlockSpec((ns, tt, D), lambda i, pos: (i // n_tt, i % n_tt, 0)),
            scratch_shapes=[pltpu.VMEM((2, TOP_K, rows * ROW_TILES, 128), F32),
                            pltpu.SemaphoreType.DMA((2,))]),
        out_shape=jax.ShapeDtypeStruct((bsz, t_len, D), F32),
        compiler_params=pltpu.CompilerParams(dimension_semantics=("arbitrary",),
                                             vmem_limit_bytes=VMEM_LIMIT),
        name="combine_prompt" if ns == 1 else "combine_state",
    )(pos_flat, x1, topw, mod3, final_g.reshape(1, 1, D), ys)


def kernel(x_prompt, x_sample, c_prompt, c_sample, state_hgrn, state_gla, ada_w, ada_b, norm1_g, norm2_g, w_in, hgrn_lb, hgrn_norm_g, gla_gate_w2, gla_gate_b, gla_norm_g, w_branch_h, w_branch_g, w_out, router_w, router_b, exp_w_gu, exp_b_gu, exp_w_dn, exp_b_dn, final_g):
    bp, tp, _ = x_prompt.shape
    bs, ts, _ = x_sample.shape
    n_prompt, n_sample = bp * tp, bs * ts
    n_tok = n_prompt + n_sample

    wi = w_in[0]
    o_gr = C_GR
    o_gg = o_gr + GLA_RANK
    win_p = jnp.concatenate([
        wi[:, :o_gr],
        jnp.pad(wi[:, o_gr:o_gg], ((0, 0), (0, 128 - GLA_RANK))),
        wi[:, o_gg:],
    ], axis=1).astype(BF16)
    w2_p = jnp.pad(gla_gate_w2[0], ((0, 128 - GLA_RANK), (0, 0))).astype(BF16)
    gb_p = gla_gate_b[0].reshape(1, GW)
    weights = [norm1_g[0].reshape(1, 1, D), win_p, hgrn_lb.astype(F32),
               hgrn_norm_g[0].reshape(1, HEAD_W), w2_p, gb_p, gla_norm_g[0].reshape(1, HEAD_W),
               w_branch_h[0].astype(BF16), w_branch_g[0].astype(BF16), w_out[0].astype(BF16),
               norm2_g[0].reshape(1, 1, D), router_w[0].T, router_b[0].reshape(N_EXPERTS, 1)]

    mod = _modulation(jnp.concatenate([c_prompt, c_sample], axis=0), ada_w[0], ada_b[0])
    mod_p = mod[:bp].reshape(bp, 1, 6 * D)
    mod_s = mod[bp:].reshape(bs, 1, 6 * D)

    x1, h2, topi, topw, hs_p, gs_p = _mixer(
        x_prompt, mod_p, weights, None, None,
        ns=1, tt=PROMPT_TILE, chunk=GROUP, row_block0=0, n_tok_total=n_tok)
    ns_s = GROUP // ts
    x1, h2, topi, topw, hs_s, gs_s = _mixer(
        x_sample, mod_s, weights, (state_hgrn, state_gla), (x1, h2, topi, topw),
        ns=ns_s, tt=ts, chunk=ts, row_block0=n_prompt // GROUP, n_tok_total=n_tok)

    n_tiles = n_tok * TOP_K // TM + N_EXPERTS
    n_tile_pad = -(-n_tiles // 128) * 128
    pos, te, tv, na, tf, tn, ts_ = _route(topi, n_tile_pad)
    pos_flat = pos.reshape(TOP_K * n_tok)
    xs = _scatter_rows(pos_flat, h2, n_tiles * TM)
    plan = (te.reshape(n_tile_pad), tv.reshape(n_tile_pad), na.reshape(128)[:1],
            tf.reshape(n_tile_pad), tn.reshape(n_tile_pad), ts_.reshape(n_tile_pad))
    ys = _experts(plan, xs, exp_w_gu[0], exp_b_gu[0], exp_w_dn[0], exp_b_dn[0])
    y_p = _combine(pos_flat, x1, topw, mod_p, final_g, ys,
                   bsz=bp, t_len=tp, ns=1, tt=256, tok0=0)
    y_s = _combine(pos_flat, x1, topw, mod_s, final_g, ys,
                   bsz=bs, t_len=ts, ns=ns_s, tt=ts, tok0=n_prompt)
    return (y_p, y_s, hs_p, gs_p, hs_s, gs_s)
```
